```python
import jax, jax.numpy as jnp
from jax import lax
import numpy as np

D_MODEL = 1024
BATCH = 4
SEQ = 8192
DEPTH = 4

N_MIXERS = 3
N_MLA = (DEPTH + 2) // 3
N_CONV = (DEPTH + 1) // 3
N_HGRN = DEPTH // 3
PLE_DIM = 256
D_FF = ((8 * D_MODEL // 3 + 255) // 256) * 256
MLA_HEADS = 16
MLA_NOPE = 64
MLA_ROPE = 32
MLA_V = 64
MLA_Q_LORA = 384
MLA_KV_LORA = 256
ROPE_BASE = 10000.0
Q_BLOCK = 128
CONV_CH = D_MODEL
CONV_WIDTH = 31
HGRN_HEADS = 8
HGRN_HEAD_DIM = D_MODEL // HGRN_HEADS
HGRN_DIM = HGRN_HEADS * HGRN_HEAD_DIM
CHUNK = 64
EPS = 1e-6

kernel_name = 'hybrid_mla_conformer_hgrn2_trunk'


def _rmsnorm(x, g):
    xf = x.astype(jnp.float32)
    y = xf * lax.rsqrt(jnp.mean(xf * xf, axis=-1, keepdims=True) + EPS)
    return (y * g.astype(jnp.float32)).astype(x.dtype)


def _layernorm(x, g, b):
    xf = x.astype(jnp.float32)
    mu = jnp.mean(xf, axis=-1, keepdims=True)
    var = jnp.mean(jnp.square(xf - mu), axis=-1, keepdims=True)
    y = (xf - mu) * lax.rsqrt(var + EPS)
    return (y * g.astype(jnp.float32) + b.astype(jnp.float32)).astype(x.dtype)


def _rope_tables(positions, dtype):
    inv_freq = 1.0 / (ROPE_BASE ** (jnp.arange(0, MLA_ROPE, 2, dtype=jnp.float32) / MLA_ROPE))
    ang = positions.astype(jnp.float32)[..., None] * inv_freq
    return jnp.cos(ang).astype(dtype), jnp.sin(ang).astype(dtype)


def _rope(x, cos, sin):
    x1, x2 = jnp.split(x, 2, axis=-1)
    return jnp.concatenate([x1 * cos - x2 * sin, x2 * cos + x1 * sin], axis=-1)


def _causal_attention(q, k, v, scale):
    B, S, H, Dk = q.shape
    Dv = v.shape[-1]
    nb = S // Q_BLOCK
    qb = q.reshape(B, nb, Q_BLOCK, H, Dk).transpose(1, 0, 3, 2, 4)
    key_pos = jnp.arange(S)

    def one_block(args):
        qi, bi = args
        s = jnp.einsum('bhqd,bkhd->bhqk', qi, k).astype(jnp.float32) * scale
        q_pos = bi * Q_BLOCK + jnp.arange(Q_BLOCK)
        s = jnp.where(key_pos[None, :] <= q_pos[:, None], s, -jnp.inf)
        pr = jax.nn.softmax(s, axis=-1).astype(v.dtype)
        return jnp.einsum('bhqk,bkhd->bqhd', pr, v)

    out = lax.map(one_block, (qb, jnp.arange(nb)))
    return out.transpose(1, 0, 2, 3, 4).reshape(B, S, H, Dv)


def _mla(u, positions, w_in, q_norm_g, w_uq, kv_norm_g, w_ukv, w_out):
    B, S, _ = u.shape
    c = u @ w_in
    c_q, c_kv, k_r = jnp.split(c, [MLA_Q_LORA, MLA_Q_LORA + MLA_KV_LORA], axis=-1)
    q = (_rmsnorm(c_q, q_norm_g) @ w_uq).reshape(B, S, MLA_HEADS, MLA_NOPE + MLA_ROPE)
    kv = (_rmsnorm(c_kv, kv_norm_g) @ w_ukv).reshape(B, S, MLA_HEADS, MLA_NOPE + MLA_V)
    q_nope, q_rope = jnp.split(q, [MLA_NOPE], axis=-1)
    k_nope, v = jnp.split(kv, [MLA_NOPE], axis=-1)
    cos, sin = _rope_tables(positions, u.dtype)
    q_rope = _rope(q_rope, cos[:, :, None, :], sin[:, :, None, :])
    k_rope = _rope(k_r, cos, sin)
    q = jnp.concatenate([q_nope, q_rope], axis=-1)
    k = jnp.concatenate([k_nope, jnp.broadcast_to(k_rope[:, :, None, :], (B, S, MLA_HEADS, MLA_ROPE))], axis=-1)
    o = _causal_attention(q, k, v, (MLA_NOPE + MLA_ROPE) ** -0.5)
    return o.reshape(B, S, MLA_HEADS * MLA_V) @ w_out


def _conformer_conv(u, w_pw1, b_pw1, w_dw, b_dw, ln_g, ln_b, w_pw2, b_pw2):
    a = u @ w_pw1 + b_pw1
    a = a[..., :CONV_CH] * jax.nn.sigmoid(a[..., CONV_CH:])
    a = lax.conv_general_dilated(a, w_dw[:, None, :].astype(a.dtype), window_strides=(1,),
                                 padding=[(CONV_WIDTH - 1, 0)],
                                 dimension_numbers=('NWC', 'WIO', 'NWC'),
                                 feature_group_count=CONV_CH) + b_dw
    a = jax.nn.silu(_layernorm(a, ln_g, ln_b))
    return a @ w_pw2 + b_pw2


def _gla_chunk_scan(q, k, v, log_f):
    B, S, H, K = q.shape
    V = v.shape[-1]
    n = S // CHUNK

    def to_chunks(t):
        return t.reshape(B, n, CHUNK, H, t.shape[-1]).transpose(1, 0, 3, 2, 4)

    qc, kc, vc = to_chunks(q), to_chunks(k), to_chunks(v)
    bc = jnp.cumsum(to_chunks(log_f), axis=3)
    causal = jnp.tril(jnp.ones((CHUNK, CHUNK), dtype=bool))

    def step(state, xs):
        qi, ki, vi, bi = xs
        diff = bi[:, :, :, None, :] - bi[:, :, None, :, :]
        decay = jnp.exp(jnp.where(causal[:, :, None], diff, -jnp.inf))
        attn = jnp.einsum('bhtsk,bhsk->bhts', qi[:, :, :, None, :] * decay, ki)
        o = jnp.einsum('bhts,bhsv->bhtv', attn, vi) + jnp.einsum('bhtk,bhkv->bhtv', qi * jnp.exp(bi), state)
        b_last = bi[:, :, -1:, :]
        new_state = jnp.exp(b_last[:, :, 0, :])[..., None] * state + \
            jnp.einsum('bhsk,bhsv->bhkv', ki * jnp.exp(b_last - bi), vi)
        return new_state, o

    s0 = jnp.zeros((B, H, K, V), jnp.float32)
    _, o = lax.scan(step, s0, (qc, kc, vc, bc))
    return o.transpose(1, 0, 3, 2, 4).reshape(B, S, H, V)


def _hgrn2(u, lb, w_in, norm_g, w_out):
    B, S, _ = u.shape
    proj = u @ w_in
    q, fz, i_in, g = jnp.split(proj, 4, axis=-1)
    lbf = lb.astype(jnp.float32)
    log_f = jnp.logaddexp(jnp.log(lbf), jnp.log1p(-lbf) + jax.nn.log_sigmoid(fz.astype(jnp.float32)))
    k = -jnp.expm1(log_f)
    hs = (B, S, HGRN_HEADS, HGRN_HEAD_DIM)
    o = _gla_chunk_scan(q.astype(jnp.float32).reshape(hs), k.reshape(hs),
                        i_in.astype(jnp.float32).reshape(hs), log_f.reshape(hs))
    o = _rmsnorm(o, norm_g.reshape(HGRN_HEADS, HGRN_HEAD_DIM)).reshape(B, S, HGRN_DIM).astype(u.dtype)
    o = o * jax.nn.sigmoid(g)
    return o @ w_out


def _swiglu(u, w_gu, w_down):
    gate, up = jnp.split(u @ w_gu, 2, axis=-1)
    return (jax.nn.silu(gate) * up) @ w_down


def setup_inputs(seed: int = 0) -> dict:
    key = jax.random.key(seed)
    ks = iter(jax.random.split(key, 40))

    def nrm(shape, scale):
        return jax.random.normal(next(ks), shape, jnp.float32) * scale

    def gain(shape):
        return 1.0 + nrm(shape, 0.02)

    D = D_MODEL
    out_scale = (2.0 * DEPTH) ** -0.5
    x = nrm((BATCH, SEQ, D), 1.0)
    p = nrm((DEPTH, BATCH, SEQ, PLE_DIM), 1.0)
    offsets = jax.random.randint(next(ks), (BATCH, 1), 0, 4096, dtype=jnp.int32)
    positions = offsets + jnp.arange(SEQ, dtype=jnp.int32)[None, :]
    return {
        'x': x, 'p': p, 'positions': positions,
        'norm1_g': gain((DEPTH, D)), 'norm2_g': gain((DEPTH, D)),
        'mla_w_in': nrm((N_MLA, D, MLA_Q_LORA + MLA_KV_LORA + MLA_ROPE), D ** -0.5),
        'mla_q_norm_g': gain((N_MLA, MLA_Q_LORA)),
        'mla_w_uq': nrm((N_MLA, MLA_Q_LORA, MLA_HEADS * (MLA_NOPE + MLA_ROPE)), MLA_Q_LORA ** -0.5),
        'mla_kv_norm_g': gain((N_MLA, MLA_KV_LORA)),
        'mla_w_ukv': nrm((N_MLA, MLA_KV_LORA, MLA_HEADS * (MLA_NOPE + MLA_V)), MLA_KV_LORA ** -0.5),
        'mla_w_out': nrm((N_MLA, MLA_HEADS * MLA_V, D), (MLA_HEADS * MLA_V) ** -0.5 * out_scale),
        'conv_w_pw1': nrm((N_CONV, D, 2 * CONV_CH), D ** -0.5),
        'conv_b_pw1': nrm((N_CONV, 2 * CONV_CH), 0.01),
        'conv_w_dw': nrm((N_CONV, CONV_WIDTH, CONV_CH), CONV_WIDTH ** -0.5),
        'conv_b_dw': nrm((N_CONV, CONV_CH), 0.01),
        'conv_ln_g': gain((N_CONV, CONV_CH)),
        'conv_ln_b': nrm((N_CONV, CONV_CH), 0.01),
        'conv_w_pw2': nrm((N_CONV, CONV_CH, D), CONV_CH ** -0.5 * out_scale),
        'conv_b_pw2': nrm((N_CONV, D), 0.01),
        'hgrn_w_in': nrm((N_HGRN, D, 4 * HGRN_DIM), D ** -0.5),
        'hgrn_lb_logits': nrm((DEPTH, HGRN_DIM), 1.0),
        'hgrn_norm_g': gain((N_HGRN, HGRN_DIM)),
        'hgrn_w_out': nrm((N_HGRN, HGRN_DIM, D), HGRN_DIM ** -0.5 * out_scale),
        'ffn_w_gu': nrm((DEPTH, D, 2 * D_FF), D ** -0.5),
        'ffn_w_down': nrm((DEPTH, D_FF, D), D_FF ** -0.5 * out_scale),
        'ple_w_proj': nrm((DEPTH, PLE_DIM, D), PLE_DIM ** -0.5 * out_scale),
        'ple_norm_g': gain((DEPTH, D)),
        'ple_w_gate': nrm((DEPTH, D, D), D ** -0.5),
        'final_norm_g': gain((D,)),
    }


def reference(x, p, positions, norm1_g, norm2_g,
              mla_w_in, mla_q_norm_g, mla_w_uq, mla_kv_norm_g, mla_w_ukv, mla_w_out,
              conv_w_pw1, conv_b_pw1, conv_w_dw, conv_b_dw, conv_ln_g, conv_ln_b, conv_w_pw2, conv_b_pw2,
              hgrn_w_in, hgrn_lb_logits, hgrn_norm_g, hgrn_w_out,
              ffn_w_gu, ffn_w_down, ple_w_proj, ple_norm_g, ple_w_gate, final_norm_g):
    lb_cum = jnp.cumsum(jax.nn.softmax(hgrn_lb_logits.astype(jnp.float32), axis=0), axis=0)
    lower_bounds = lb_cum - lb_cum[0:1]
    h = x
    for i in range(DEPTH):
        mixer = i % N_MIXERS
        j = i // N_MIXERS
        u = _rmsnorm(h, norm1_g[i])
        if mixer == 0:
            h = h + _mla(u, positions, mla_w_in[j], mla_q_norm_g[j], mla_w_uq[j],
                         mla_kv_norm_g[j], mla_w_ukv[j], mla_w_out[j])
        elif mixer == 1:
            h = h + _conformer_conv(u, conv_w_pw1[j], conv_b_pw1[j], conv_w_dw[j], conv_b_dw[j],
                                    conv_ln_g[j], conv_ln_b[j], conv_w_pw2[j], conv_b_pw2[j])
        else:
            h = h + _hgrn2(u, lower_bounds[i], hgrn_w_in[j], hgrn_norm_g[j], hgrn_w_out[j])
        h = h + _swiglu(_rmsnorm(h, norm2_g[i]), ffn_w_gu[i], ffn_w_down[i])
        gate = jax.nn.sigmoid(_rmsnorm(h, ple_norm_g[i]) @ ple_w_gate[i])
        h = h + (p[i] @ ple_w_proj[i]) * gate
    return _rmsnorm(h, final_norm_g)
```

```python
import functools
import math

import jax
import jax.numpy as jnp
from jax import lax
from jax.experimental import pallas as pl
from jax.experimental.pallas import tpu as pltpu

D_MODEL = 1024
N_MIXERS = 3
PLE_DIM = 256
D_FF = 2816
MLA_HEADS = 16
MLA_NOPE = 64
MLA_ROPE = 32
MLA_V = 64
MLA_Q_LORA = 384
MLA_KV_LORA = 256
ROPE_BASE = 10000.0
CONV_WIDTH = 31
HGRN_HEADS = 8
HGRN_HEAD_DIM = 128
CHUNK = 64
EPS = 1e-6

LANES = 128
HEAD_PAD = 128
LOG2E = 1.4426950408889634
F32 = jnp.float32
BF16 = jnp.bfloat16
NEG_INF = float("-inf")
VMEM_LIMIT = 56 * 1024 * 1024


def _cparams(*sem):
    return pltpu.CompilerParams(dimension_semantics=sem, vmem_limit_bytes=VMEM_LIMIT)


def _rms(x, g):
    return x * lax.rsqrt(jnp.mean(x * x, axis=-1, keepdims=True) + EPS) * g


def _dot(a, b):
    return jnp.dot(a, b, preferred_element_type=F32)


def _dot_nt(a, b):
    return lax.dot_general(a, b, (((1,), (1,)), ((), ())), preferred_element_type=F32)


def _dot_tn(a, b):
    return lax.dot_general(a, b, (((0,), (0,)), ((), ())), preferred_element_type=F32)


def _sigmoid(x):
    return 1.0 / (1.0 + jnp.exp(-x))


def _rope_table_kernel(pos_ref, invf_ref, sign_ref, cos_ref, sin_ref):
    ang = pos_ref[0].astype(F32) * invf_ref[...]
    cos_ref[0] = jnp.cos(ang)
    sin_ref[0] = jnp.sin(ang) * sign_ref[...]


def _rope_tables(positions, ts):
    B, S = positions.shape
    inv_freq = 1.0 / (ROPE_BASE ** (jnp.arange(0, MLA_ROPE, 2, dtype=F32) / MLA_ROPE))
    half = MLA_ROPE // 2
    zeros_n = jnp.zeros((MLA_NOPE,), F32)
    zeros_p = jnp.zeros((HEAD_PAD - MLA_NOPE - MLA_ROPE,), F32)
    invf = jnp.concatenate([zeros_n, inv_freq, inv_freq, zeros_p])[None, :]
    sign = jnp.concatenate([zeros_n, -jnp.ones((half,), F32), jnp.ones((half,), F32), zeros_p])[None, :]
    vec = pl.BlockSpec((1, HEAD_PAD), lambda b, s: (0, 0))
    tab = pl.BlockSpec((1, ts, HEAD_PAD), lambda b, s: (b, s, 0))
    return pl.pallas_call(
        _rope_table_kernel,
        out_shape=(jax.ShapeDtypeStruct((B, S, HEAD_PAD), F32),) * 2,
        grid=(B, S // ts),
        in_specs=[pl.BlockSpec((1, ts, 1), lambda b, s: (b, s, 0)), vec, vec],
        out_specs=(tab, tab),
        compiler_params=_cparams("parallel", "parallel"),
        name="rope_tables",
    )(positions[:, :, None], invf, sign)


def _rope_apply(x, cos, sin_signed, lane):
    half = MLA_ROPE // 2
    from_hi = pltpu.roll(x, HEAD_PAD - half, axis=1)
    from_lo = pltpu.roll(x, half, axis=1)
    swapped = jnp.where(lane < MLA_NOPE + half, from_hi, from_lo)
    return x * cos + swapped * sin_signed


def _mla_proj_kernel(h_ref, g1_ref, wc_ref, gq_ref, gkv_ref, wuq_ref, wuk_ref, wuv_ref,
                     cos_ref, sin_ref, q_ref, k_ref, v_ref, *, q_scale):
    xn = _rms(h_ref[0], g1_ref[...]).astype(BF16)
    c = _dot(xn, wc_ref[...])
    cq = c[:, :MLA_Q_LORA]
    ckv = c[:, MLA_Q_LORA:MLA_Q_LORA + MLA_KV_LORA]
    kr = c[:, MLA_Q_LORA + MLA_KV_LORA:]
    cqn = _rms(cq, gq_ref[...]).astype(BF16)
    ckvn = _rms(ckv, gkv_ref[...]).astype(BF16)
    cos = cos_ref[0]
    sin = sin_ref[0]
    lane = lax.broadcasted_iota(jnp.int32, cos.shape, 1)
    kr_rot = _rope_apply(kr, cos, sin, lane)
    qf = _dot(cqn, wuq_ref[...])
    kf = _dot(ckvn, wuk_ref[...])
    for hh in range(MLA_HEADS):
        sl = slice(hh * HEAD_PAD, (hh + 1) * HEAD_PAD)
        q_ref[0, hh] = (_rope_apply(qf[:, sl], cos, sin, lane) * q_scale).astype(BF16)
        k_ref[0, hh] = (kf[:, sl] + kr_rot).astype(BF16)
    v_ref[0] = _dot(ckvn, wuv_ref[...]).astype(BF16)


def _mla_proj(h, g1, w_in, gq, w_uq, gkv, w_ukv, cos, sin, ts):
    B, S, D = h.shape
    nq, nkv = MLA_Q_LORA, MLA_KV_LORA
    pad = HEAD_PAD - MLA_NOPE - MLA_ROPE
    w_kr = jnp.pad(w_in[:, nq + nkv:], ((0, 0), (MLA_NOPE, pad)))
    wc = jnp.concatenate([w_in[:, :nq + nkv], w_kr], axis=1).astype(BF16)
    wuq = jnp.pad(w_uq.reshape(nq, MLA_HEADS, MLA_NOPE + MLA_ROPE), ((0, 0), (0, 0), (0, pad)))
    wuq = wuq.reshape(nq, MLA_HEADS * HEAD_PAD).astype(BF16)
    wukv = w_ukv.reshape(nkv, MLA_HEADS, MLA_NOPE + MLA_V)
    wuk = jnp.pad(wukv[:, :, :MLA_NOPE], ((0, 0), (0, 0), (0, HEAD_PAD - MLA_NOPE)))
    wuk = wuk.reshape(nkv, MLA_HEADS * HEAD_PAD).astype(BF16)
    wuv = wukv[:, :, MLA_NOPE:].reshape(nkv, MLA_HEADS * MLA_V).astype(BF16)
    q_scale = (MLA_NOPE + MLA_ROPE) ** -0.5 * LOG2E

    def full(a):
        return pl.BlockSpec(a.shape, lambda b, s: (0,) * a.ndim)

    g1, gq, gkv = g1[None, :], gq[None, :], gkv[None, :]
    tab = pl.BlockSpec((1, ts, HEAD_PAD), lambda b, s: (b, s, 0))
    head_spec = pl.BlockSpec((1, MLA_HEADS, ts, HEAD_PAD), lambda b, s: (b, 0, s, 0))
    qk_shape = jax.ShapeDtypeStruct((B, MLA_HEADS, S, HEAD_PAD), BF16)
    return pl.pallas_call(
        functools.partial(_mla_proj_kernel, q_scale=q_scale),
        out_shape=(qk_shape, qk_shape, jax.ShapeDtypeStruct((B, S, MLA_HEADS * MLA_V), BF16)),
        grid=(B, S // ts),
        in_specs=[pl.BlockSpec((1, ts, D), lambda b, s: (b, s, 0)), full(g1), full(wc), full(gq),
                  full(gkv), full(wuq), full(wuk), full(wuv), tab, tab],
        out_specs=(head_spec, head_spec,
                   pl.BlockSpec((1, ts, MLA_HEADS * MLA_V), lambda b, s: (b, s, 0))),
        compiler_params=_cparams("parallel", "parallel"),
        name="mla_proj",
    )(h, g1, wc, gq, gkv, wuq, wuk, wuv, cos, sin)


def _attn_kernel(q_ref, k_ref, v_ref, o_ref, m_scr, l_scr, acc_scr, *, tq):
    qi = pl.program_id(2)
    m_scr[...] = jnp.full(m_scr.shape, NEG_INF, F32)
    l_scr[...] = jnp.zeros(l_scr.shape, F32)
    acc_scr[...] = jnp.zeros(acc_scr.shape, F32)

    def block(j, masked):
        start = pl.multiple_of(j * tq, tq)
        v = v_ref[0, pl.ds(start, tq), :]
        for hh in range(2):
            s = _dot_nt(q_ref[0, hh], k_ref[0, hh, pl.ds(start, tq), :])
            if masked:
                row = lax.broadcasted_iota(jnp.int32, s.shape, 0)
                col = lax.broadcasted_iota(jnp.int32, s.shape, 1)
                s = jnp.where(col <= row, s, NEG_INF)
            m_prev = m_scr[hh]
            m_new = jnp.maximum(m_prev, jnp.max(s, axis=1, keepdims=True))
            alpha = jnp.exp2(m_prev - m_new)
            p = jnp.exp2(s - m_new)
            l_scr[hh] = alpha * l_scr[hh] + jnp.sum(p, axis=1, keepdims=True)
            acc_scr[hh] = alpha * acc_scr[hh] + _dot(p.astype(BF16), v)
            m_scr[hh] = m_new

    def body(j, carry):
        block(j, False)
        return carry

    lax.fori_loop(0, qi, body, 0)
    block(qi, True)
    lane = lax.broadcasted_iota(jnp.int32, (tq, LANES), 1)
    o_even = acc_scr[0] / l_scr[0]
    o_odd = acc_scr[1] / l_scr[1]
    o_ref[0] = jnp.where(lane < MLA_V, o_even, o_odd).astype(BF16)


def _attention(q, k, v, tq):
    B, H, S, _ = q.shape
    return pl.pallas_call(
        functools.partial(_attn_kernel, tq=tq),
        out_shape=jax.ShapeDtypeStruct((B, S, H * MLA_V), BF16),
        grid=(B, H // 2, S // tq),
        in_specs=[pl.BlockSpec((1, 2, tq, HEAD_PAD), lambda b, p, i: (b, p, i, 0)),
                  pl.BlockSpec((1, 2, S, HEAD_PAD), lambda b, p, i: (b, p, 0, 0)),
                  pl.BlockSpec((1, S, 2 * MLA_V), lambda b, p, i: (b, 0, p))],
        out_specs=pl.BlockSpec((1, tq, 2 * MLA_V), lambda b, p, i: (b, i, p)),
        scratch_shapes=[pltpu.VMEM((2, tq, 1), F32), pltpu.VMEM((2, tq, 1), F32),
                        pltpu.VMEM((2, tq, 2 * MLA_V), F32)],
        compiler_params=_cparams("parallel", "parallel", "arbitrary"),
        name="mla_attention",
    )(q, k, v)


def _mm_res_kernel(a_ref, w_ref, b_ref, h_ref, o_ref):
    o_ref[...] = h_ref[...] + _dot(a_ref[...].astype(BF16), w_ref[...]) + b_ref[...]


def _mm_res(a, w, bias, h, tm):
    T, K = a.shape
    N = w.shape[1]
    return pl.pallas_call(
        _mm_res_kernel,
        out_shape=jax.ShapeDtypeStruct((T, N), F32),
        grid=(T // tm,),
        in_specs=[pl.BlockSpec((tm, K), lambda i: (i, 0)), pl.BlockSpec((K, N), lambda i: (0, 0)),
                  pl.BlockSpec((1, N), lambda i: (0, 0)), pl.BlockSpec((tm, N), lambda i: (i, 0))],
        out_specs=pl.BlockSpec((tm, N), lambda i: (i, 0)),
        compiler_params=_cparams("parallel"),
        name="matmul_residual",
    )(a, w.astype(BF16), bias[None, :], h)


def _norm_mm_kernel(h_ref, g_ref, w_ref, o_ref, xn_scr):
    @pl.when(pl.program_id(1) == 0)
    def _():
        xn_scr[...] = _rms(h_ref[...], g_ref[...]).astype(BF16)

    o_ref[...] = _dot(xn_scr[...], w_ref[...])


def _norm_mm(h, g, w, tm, tn):
    T, K = h.shape
    N = w.shape[1]
    return pl.pallas_call(
        _norm_mm_kernel,
        out_shape=jax.ShapeDtypeStruct((T, N), F32),
        grid=(T // tm, N // tn),
        in_specs=[pl.BlockSpec((tm, K), lambda i, j: (i, 0)), pl.BlockSpec((1, K), lambda i, j: (0, 0)),
                  pl.BlockSpec((K, tn), lambda i, j: (0, j))],
        out_specs=pl.BlockSpec((tm, tn), lambda i, j: (i, j)),
        scratch_shapes=[pltpu.VMEM((tm, K), BF16)],
        compiler_params=_cparams("parallel", "arbitrary"),
        name="norm_matmul",
    )(h, g[None, :], w.astype(BF16))


def _ffn_kernel(h_ref, g_ref, wg_ref, wu_ref, wd_ref, o_ref, xn_scr, acc_scr):
    f = pl.program_id(1)

    @pl.when(f == 0)
    def _():
        xn_scr[...] = _rms(h_ref[...], g_ref[...]).astype(BF16)
        acc_scr[...] = h_ref[...]

    xn = xn_scr[...]
    gate = _dot(xn, wg_ref[...])
    up = _dot(xn, wu_ref[...])
    act = (gate * _sigmoid(gate) * up).astype(BF16)
    acc_scr[...] += _dot(act, wd_ref[...])

    @pl.when(f == pl.num_programs(1) - 1)
    def _():
        o_ref[...] = acc_scr[...]


def _ffn(h, g, w_gu, w_down, tm, tf):
    T, D = h.shape
    nf = D_FF // tf
    w_gu = w_gu.astype(BF16)
    return pl.pallas_call(
        _ffn_kernel,
        out_shape=jax.ShapeDtypeStruct((T, D), F32),
        grid=(T // tm, nf),
        in_specs=[pl.BlockSpec((tm, D), lambda i, f: (i, 0)), pl.BlockSpec((1, D), lambda i, f: (0, 0)),
                  pl.BlockSpec((D, tf), lambda i, f: (0, f)),
                  pl.BlockSpec((D, tf), lambda i, f: (0, f + nf)),
                  pl.BlockSpec((tf, D), lambda i, f: (f, 0))],
        out_specs=pl.BlockSpec((tm, D), lambda i, f: (i, 0)),
        scratch_shapes=[pltpu.VMEM((tm, D), BF16), pltpu.VMEM((tm, D), F32)],
        compiler_params=_cparams("parallel", "arbitrary"),
        name="swiglu_ffn",
    )(h, g[None, :], w_gu, w_gu, w_down.astype(BF16))


def _ple_kernel(h_ref, p_ref, g_ref, wg_ref, wp_ref, gf_ref, o_ref, *, final_norm):
    h = h_ref[...]
    gate = _sigmoid(_dot(_rms(h, g_ref[...]).astype(BF16), wg_ref[...]))
    out = h + _dot(p_ref[...].astype(BF16), wp_ref[...]) * gate
    if final_norm:
        out = _rms(out, gf_ref[...])
    o_ref[...] = out


def _ple(h, p, g, w_gate, w_proj, g_final, final_norm, tm):
    T, D = h.shape
    return pl.pallas_call(
        functools.partial(_ple_kernel, final_norm=final_norm),
        out_shape=jax.ShapeDtypeStruct((T, D), F32),
        grid=(T // tm,),
        in_specs=[pl.BlockSpec((tm, D), lambda i: (i, 0)), pl.BlockSpec((tm, PLE_DIM), lambda i: (i, 0)),
                  pl.BlockSpec((1, D), lambda i: (0, 0)), pl.BlockSpec((D, D), lambda i: (0, 0)),
                  pl.BlockSpec((PLE_DIM, D), lambda i: (0, 0)), pl.BlockSpec((1, D), lambda i: (0, 0))],
        out_specs=pl.BlockSpec((tm, D), lambda i: (i, 0)),
        compiler_params=_cparams("parallel"),
        name="per_layer_embedding",
    )(h, p, g[None, :], w_gate.astype(BF16), w_proj.astype(BF16), g_final[None, :])


def _glu_kernel(h_ref, g_ref, wa_ref, wb_ref, ba_ref, bb_ref, o_ref, xn_scr):
    @pl.when(pl.program_id(1) == 0)
    def _():
        xn_scr[...] = _rms(h_ref[...], g_ref[...]).astype(BF16)

    xn = xn_scr[...]
    a = _dot(xn, wa_ref[...]) + ba_ref[...]
    b = _dot(xn, wb_ref[...]) + bb_ref[...]
    o_ref[...] = a * _sigmoid(b)


def _conv_glu(h, g, w_pw1, b_pw1, tm, tn):
    T, D = h.shape
    nn = D // tn
    w = w_pw1.astype(BF16)
    b = b_pw1[None, :]
    return pl.pallas_call(
        _glu_kernel,
        out_shape=jax.ShapeDtypeStruct((T, D), F32),
        grid=(T // tm, nn),
        in_specs=[pl.BlockSpec((tm, D), lambda i, j: (i, 0)), pl.BlockSpec((1, D), lambda i, j: (0, 0)),
                  pl.BlockSpec((D, tn), lambda i, j: (0, j)), pl.BlockSpec((D, tn), lambda i, j: (0, j + nn)),
                  pl.BlockSpec((1, tn), lambda i, j: (0, j)), pl.BlockSpec((1, tn), lambda i, j: (0, j + nn))],
        out_specs=pl.BlockSpec((tm, tn), lambda i, j: (i, j)),
        scratch_shapes=[pltpu.VMEM((tm, D), BF16)],
        compiler_params=_cparams("parallel", "arbitrary"),
        name="conv_glu",
    )(h, g[None, :], w, w, b, b)


CONV_HALO = 32
CONV_ROWS = 64


def _conv_tail_kernel(a_ref, h_ref, wdw_ref, bdw_ref, lng_ref, lnb_ref, w2_ref, b2_ref, o_ref,
                      ext_scr, y_scr, *, ts):
    @pl.when(pl.program_id(1) == 0)
    def _():
        ext_scr[0:CONV_HALO, :] = jnp.zeros((CONV_HALO, D_MODEL), F32)

    ext_scr[CONV_HALO:, :] = a_ref[0]
    off = CONV_HALO - (CONV_WIDTH - 1)
    for r0 in range(0, ts, CONV_ROWS):
        for c0 in range(0, D_MODEL, LANES):
            acc = jnp.zeros((CONV_ROWS, LANES), F32)
            for j in range(CONV_WIDTH):
                acc = acc + wdw_ref[j:j + 1, c0:c0 + LANES] * ext_scr[r0 + off + j:r0 + off + j + CONV_ROWS, c0:c0 + LANES]
            y_scr[r0:r0 + CONV_ROWS, c0:c0 + LANES] = acc
    ext_scr[0:CONV_HALO, :] = ext_scr[ts:ts + CONV_HALO, :]
    y = y_scr[...] + bdw_ref[...]
    mu = jnp.mean(y, axis=-1, keepdims=True)
    yc = y - mu
    var = jnp.mean(yc * yc, axis=-1, keepdims=True)
    z = yc * lax.rsqrt(var + EPS) * lng_ref[...] + lnb_ref[...]
    z = (z * _sigmoid(z)).astype(BF16)
    o_ref[0] = h_ref[0] + _dot(z, w2_ref[...]) + b2_ref[...]


def _conv_tail(a, h, w_dw, b_dw, ln_g, ln_b, w_pw2, b_pw2, ts):
    B, S, D = h.shape

    def full(x):
        return pl.BlockSpec(x.shape, lambda b, s: (0,) * x.ndim)

    tile = pl.BlockSpec((1, ts, D), lambda b, s: (b, s, 0))
    args = (w_dw, b_dw[None, :], ln_g[None, :], ln_b[None, :], w_pw2.astype(BF16), b_pw2[None, :])
    return pl.pallas_call(
        functools.partial(_conv_tail_kernel, ts=ts),
        out_shape=jax.ShapeDtypeStruct((B, S, D), F32),
        grid=(B, S // ts),
        in_specs=[tile, tile] + [full(x) for x in args],
        out_specs=tile,
        scratch_shapes=[pltpu.VMEM((ts + CONV_HALO, D), F32), pltpu.VMEM((ts, D), F32)],
        compiler_params=_cparams("parallel", "arbitrary"),
        name="conv_tail",
    )(a, h, *args)


def _lower_bounds_kernel(x_ref, o_ref):
    x = x_ref[...]
    n = x.shape[0]
    m = x[0:1]
    for r in range(1, n):
        m = jnp.maximum(m, x[r:r + 1])
    e = jnp.exp(x - m)
    tot = e[0:1]
    for r in range(1, n):
        tot = tot + e[r:r + 1]
    sm = e / tot
    cum = sm[0:1]
    first = cum
    o_ref[0:1, :] = cum - first
    for r in range(1, n):
        cum = cum + sm[r:r + 1]
        o_ref[r:r + 1, :] = cum - first


def _lower_bounds(logits):
    return pl.pallas_call(
        _lower_bounds_kernel,
        out_shape=jax.ShapeDtypeStruct(logits.shape, F32),
        name="hgrn_lower_bounds",
    )(logits)


def _cumsum_rows(x):
    row = lax.broadcasted_iota(jnp.int32, x.shape, 0)
    step = 1
    while step < x.shape[0]:
        x = x + jnp.where(row >= step, pltpu.roll(x, step, axis=0), 0.0)
        step *= 2
    return x


def _bcast_rows(b, idx):
    return jnp.concatenate([jnp.broadcast_to(b[i:i + 1, :], (8, b.shape[1])) for i in idx], axis=0)


def _hgrn_chunk(q, fz, v, lb, st):
    C = CHUNK
    log_sig = jnp.minimum(fz, 0.0) - jnp.log1p(jnp.exp(-jnp.abs(fz)))
    t1 = jnp.log(lb)
    t2 = jnp.log1p(-lb) + log_sig
    hi = jnp.maximum(t1, t2)
    lf = hi + jnp.log1p(jnp.exp(-jnp.abs(t1 - t2)))
    kk = 1.0 - jnp.exp(lf)
    b = _cumsum_rows(lf)
    b_last = b[C - 1:C, :]
    row = lax.broadcasted_iota(jnp.int32, (C, LANES), 0)

    o = _dot_nt((q * jnp.exp(b)).astype(BF16), st.astype(BF16))
    kd = (kk * jnp.exp(b_last - b)).astype(BF16)
    st_new = st * jnp.exp(b_last) + _dot_tn(v.astype(BF16), kd)

    trow = lax.broadcasted_iota(jnp.int32, (C, C), 0)
    scol = lax.broadcasted_iota(jnp.int32, (C, C), 1)
    attn = jnp.zeros((C, C), F32)
    for m in (32, 16, 8):
        sh = m.bit_length() - 1
        odd = ((row >> sh) & 1) == 1
        ref_idx = []
        for g8 in range(C // 8):
            blk = (g8 * 8) // m
            ref_idx.append(blk * m - 1 if blk % 2 == 1 else (blk + 1) * m - 1)
        ref = _bcast_rows(b, ref_idx)
        e = jnp.exp(jnp.where(odd, b - ref, ref - b))
        qs = jnp.where(odd, q * e, 0.0).astype(BF16)
        ks = jnp.where(odd, 0.0, kk * e).astype(BF16)
        a_m = _dot_nt(qs, ks)
        pair = (((trow >> sh) & 1) == 1) & ((scol >> sh) == (trow >> sh) - 1)
        attn = attn + jnp.where(pair, a_m, 0.0)

    for d in range(8):
        if d == 0:
            w = q * kk
        else:
            valid = (row & 7) >= d
            diff = jnp.where(valid, b - pltpu.roll(b, d, axis=0), NEG_INF)
            w = q * pltpu.roll(kk, d, axis=0) * jnp.exp(diff)
        val = jnp.sum(w, axis=1, keepdims=True)
        attn = attn + jnp.where(trow - scol == d, val, 0.0)

    o = o + _dot(attn.astype(BF16), v.astype(BF16))
    return o, st_new


def _hgrn_scan_kernel(q_ref, fz_ref, v_ref, gate_ref, lb_ref, ng_ref, o_ref, st_scr, *, ts):
    @pl.when(pl.program_id(2) == 0)
    def _():
        st_scr[...] = jnp.zeros(st_scr.shape, F32)

    lb = lb_ref[...]
    ng = ng_ref[...]

    def body(c, carry):
        r0 = pl.multiple_of(c * CHUNK, CHUNK)
        rows = pl.ds(r0, CHUNK)
        o, st_new = _hgrn_chunk(q_ref[0, rows, :], fz_ref[0, rows, :], v_ref[0, rows, :], lb, st_scr[...])
        st_scr[...] = st_new
        o = _rms(o, ng) * _sigmoid(gate_ref[0, rows, :])
        o_ref[0, rows, :] = o.astype(BF16)
        return carry

    lax.fori_loop(0, ts // CHUNK, body, 0)


def _hgrn_scan(proj, lb, norm_g, ts):
    B, S, _ = proj.shape
    H = HGRN_HEADS

    def col(off):
        return pl.BlockSpec((1, ts, HGRN_HEAD_DIM), lambda b, h, s: (b, s, h + off))

    vec = pl.BlockSpec((1, HGRN_HEAD_DIM), lambda b, h, s: (0, h))
    return pl.pallas_call(
        functools.partial(_hgrn_scan_kernel, ts=ts),
        out_shape=jax.ShapeDtypeStruct((B, S, H * HGRN_HEAD_DIM), BF16),
        grid=(B, H, S // ts),
        in_specs=[col(0), col(H), col(2 * H), col(3 * H), vec, vec],
        out_specs=pl.BlockSpec((1, ts, HGRN_HEAD_DIM), lambda b, h, s: (b, s, h)),
        scratch_shapes=[pltpu.VMEM((HGRN_HEAD_DIM, HGRN_HEAD_DIM), F32)],
        compiler_params=_cparams("parallel", "parallel", "arbitrary"),
        name="hgrn_scan",
    )(proj, proj, proj, proj, lb, norm_g[None, :])


def _tile(n, want):
    t = min(n, want)
    assert n % t == 0, (n, want)
    return t


def kernel(x, p, positions, norm1_g, norm2_g, mla_w_in, mla_q_norm_g, mla_w_uq, mla_kv_norm_g, mla_w_ukv, mla_w_out, conv_w_pw1, conv_b_pw1, conv_w_dw, conv_b_dw, conv_ln_g, conv_ln_b, conv_w_pw2, conv_b_pw2, hgrn_w_in, hgrn_lb_logits, hgrn_norm_g, hgrn_w_out, ffn_w_gu, ffn_w_down, ple_w_proj, ple_norm_g, ple_w_gate, final_norm_g):
    B, S, D = x.shape
    depth = norm1_g.shape[0]
    T = B * S
    tm = _tile(T, 512)
    ts_proj = _tile(S, 256)
    tq = _tile(S, 512)
    ts_conv = _tile(S, 256)
    ts_scan = _tile(S, 512)
    zero_bias = jnp.zeros((D,), F32)

    lower_bounds = _lower_bounds(hgrn_lb_logits.astype(F32))
    cos, sin = _rope_tables(positions, _tile(S, 512))

    h = x
    for i in range(depth):
        mixer = i % N_MIXERS
        j = i // N_MIXERS
        if mixer == 0:
            q, k, v = _mla_proj(h, norm1_g[i], mla_w_in[j], mla_q_norm_g[j], mla_w_uq[j],
                                mla_kv_norm_g[j], mla_w_ukv[j], cos, sin, ts_proj)
            o = _attention(q, k, v, tq)
            h = _mm_res(o.reshape(T, D), mla_w_out[j], zero_bias, h.reshape(T, D), tm)
        elif mixer == 1:
            a = _conv_glu(h.reshape(T, D), norm1_g[i], conv_w_pw1[j], conv_b_pw1[j], tm, 512)
            h = _conv_tail(a.reshape(B, S, D), h.reshape(B, S, D), conv_w_dw[j], conv_b_dw[j],
                           conv_ln_g[j], conv_ln_b[j], conv_w_pw2[j], conv_b_pw2[j], ts_conv)
        else:
            proj = _norm_mm(h.reshape(T, D), norm1_g[i], hgrn_w_in[j], tm, 1024)
            o = _hgrn_scan(proj.reshape(B, S, 4 * D), lower_bounds[i:i + 1], hgrn_norm_g[j], ts_scan)
            h = _mm_res(o.reshape(T, D), hgrn_w_out[j], zero_bias, h.reshape(T, D), tm)
        h = _ffn(h.reshape(T, D), norm2_g[i], ffn_w_gu[i], ffn_w_down[i], tm, D_FF // 2)
        h = _ple(h, p[i].reshape(T, PLE_DIM), ple_norm_g[i], ple_w_gate[i], ple_w_proj[i],
                 final_norm_g, i == depth - 1, tm)
        h = h.reshape(B, S, D)
    return h
```

```python
import functools
import math

import jax
import jax.numpy as jnp
from jax import lax
from jax.experimental import pallas as pl
from jax.experimental.pallas import tpu as pltpu

D_MODEL = 1024
N_MIXERS = 3
PLE_DIM = 256
D_FF = 2816
MLA_HEADS = 16
MLA_NOPE = 64
MLA_ROPE = 32
MLA_V = 64
MLA_Q_LORA = 384
MLA_KV_LORA = 256
ROPE_BASE = 10000.0
CONV_WIDTH = 31
HGRN_HEADS = 8
HGRN_HEAD_DIM = 128
CHUNK = 64
EPS = 1e-6

LANES = 128
HEAD_PAD = 128
LOG2E = 1.4426950408889634
F32 = jnp.float32
BF16 = jnp.bfloat16
NEG_INF = float("-inf")
VMEM_LIMIT = 56 * 1024 * 1024


def _cparams(*sem):
    return pltpu.CompilerParams(dimension_semantics=sem, vmem_limit_bytes=VMEM_LIMIT)


def _rms(x, g):
    return x * lax.rsqrt(jnp.mean(x * x, axis=-1, keepdims=True) + EPS) * g


def _dot(a, b):
    return jnp.dot(a, b, preferred_element_type=F32)


def _dot_nt(a, b):
    return lax.dot_general(a, b, (((1,), (1,)), ((), ())), preferred_element_type=F32)


def _dot_tn(a, b):
    return lax.dot_general(a, b, (((0,), (0,)), ((), ())), preferred_element_type=F32)


def _sigmoid(x):
    return 1.0 / (1.0 + jnp.exp(-x))


def _rope_table_kernel(pos_ref, invf_ref, sign_ref, cos_ref, sin_ref):
    ang = pos_ref[0].astype(F32) * invf_ref[...]
    cos_ref[0] = jnp.cos(ang)
    sin_ref[0] = jnp.sin(ang) * sign_ref[...]


def _rope_tables(positions, ts):
    B, S = positions.shape
    inv_freq = 1.0 / (ROPE_BASE ** (jnp.arange(0, MLA_ROPE, 2, dtype=F32) / MLA_ROPE))
    half = MLA_ROPE // 2
    zeros_n = jnp.zeros((MLA_NOPE,), F32)
    zeros_p = jnp.zeros((HEAD_PAD - MLA_NOPE - MLA_ROPE,), F32)
    invf = jnp.concatenate([zeros_n, inv_freq, inv_freq, zeros_p])[None, :]
    sign = jnp.concatenate([zeros_n, -jnp.ones((half,), F32), jnp.ones((half,), F32), zeros_p])[None, :]
    vec = pl.BlockSpec((1, HEAD_PAD), lambda b, s: (0, 0))
    tab = pl.BlockSpec((1, ts, HEAD_PAD), lambda b, s: (b, s, 0))
    return pl.pallas_call(
        _rope_table_kernel,
        out_shape=(jax.ShapeDtypeStruct((B, S, HEAD_PAD), F32),) * 2,
        grid=(B, S // ts),
        in_specs=[pl.BlockSpec((1, ts, 1), lambda b, s: (b, s, 0)), vec, vec],
        out_specs=(tab, tab),
        compiler_params=_cparams("parallel", "parallel"),
        name="rope_tables",
    )(positions[:, :, None], invf, sign)


def _rope_apply(x, cos, sin_signed, lane):
    half = MLA_ROPE // 2
    from_hi = pltpu.roll(x, HEAD_PAD - half, axis=1)
    from_lo = pltpu.roll(x, half, axis=1)
    swapped = jnp.where(lane < MLA_NOPE + half, from_hi, from_lo)
    return x * cos + swapped * sin_signed


def _mla_proj_kernel(h_ref, g1_ref, wc_ref, gq_ref, gkv_ref, wuq_ref, wuk_ref, wuv_ref,
                     cos_ref, sin_ref, q_ref, k_ref, v_ref, *, q_scale):
    xn = _rms(h_ref[0], g1_ref[...]).astype(BF16)
    c = _dot(xn, wc_ref[...])
    cq = c[:, :MLA_Q_LORA]
    ckv = c[:, MLA_Q_LORA:MLA_Q_LORA + MLA_KV_LORA]
    kr = c[:, MLA_Q_LORA + MLA_KV_LORA:]
    cqn = _rms(cq, gq_ref[...]).astype(BF16)
    ckvn = _rms(ckv, gkv_ref[...]).astype(BF16)
    cos = cos_ref[0]
    sin = sin_ref[0]
    lane = lax.broadcasted_iota(jnp.int32, cos.shape, 1)
    kr_rot = _rope_apply(kr, cos, sin, lane)
    qf = _dot(cqn, wuq_ref[...])
    kf = _dot(ckvn, wuk_ref[...])
    for hh in range(MLA_HEADS):
        sl = slice(hh * HEAD_PAD, (hh + 1) * HEAD_PAD)
        q_ref[0, hh] = (_rope_apply(qf[:, sl], cos, sin, lane) * q_scale).astype(BF16)
        k_ref[0, hh] = (kf[:, sl] + kr_rot).astype(BF16)
    v_ref[0] = _dot_nt(wuv_ref[...], ckvn).astype(BF16)


def _mla_proj(h, g1, w_in, gq, w_uq, gkv, w_ukv, cos, sin, ts):
    B, S, D = h.shape
    nq, nkv = MLA_Q_LORA, MLA_KV_LORA
    pad = HEAD_PAD - MLA_NOPE - MLA_ROPE
    w_kr = jnp.pad(w_in[:, nq + nkv:], ((0, 0), (MLA_NOPE, pad)))
    wc = jnp.concatenate([w_in[:, :nq + nkv], w_kr], axis=1).astype(BF16)
    wuq = jnp.pad(w_uq.reshape(nq, MLA_HEADS, MLA_NOPE + MLA_ROPE), ((0, 0), (0, 0), (0, pad)))
    wuq = wuq.reshape(nq, MLA_HEADS * HEAD_PAD).astype(BF16)
    wukv = w_ukv.reshape(nkv, MLA_HEADS, MLA_NOPE + MLA_V)
    wuk = jnp.pad(wukv[:, :, :MLA_NOPE], ((0, 0), (0, 0), (0, HEAD_PAD - MLA_NOPE)))
    wuk = wuk.reshape(nkv, MLA_HEADS * HEAD_PAD).astype(BF16)
    wuv = wukv[:, :, MLA_NOPE:].reshape(nkv, MLA_HEADS * MLA_V).T.astype(BF16)
    q_scale = (MLA_NOPE + MLA_ROPE) ** -0.5 * LOG2E

    def full(a):
        return pl.BlockSpec(a.shape, lambda b, s: (0,) * a.ndim)

    g1, gq, gkv = g1[None, :], gq[None, :], gkv[None, :]
    tab = pl.BlockSpec((1, ts, HEAD_PAD), lambda b, s: (b, s, 0))
    head_spec = pl.BlockSpec((1, MLA_HEADS, ts, HEAD_PAD), lambda b, s: (b, 0, s, 0))
    qk_shape = jax.ShapeDtypeStruct((B, MLA_HEADS, S, HEAD_PAD), BF16)
    return pl.pallas_call(
        functools.partial(_mla_proj_kernel, q_scale=q_scale),
        out_shape=(qk_shape, qk_shape, jax.ShapeDtypeStruct((B, MLA_HEADS * MLA_V, S), BF16)),
        grid=(B, S // ts),
        in_specs=[pl.BlockSpec((1, ts, D), lambda b, s: (b, s, 0)), full(g1), full(wc), full(gq),
                  full(gkv), full(wuq), full(wuk), full(wuv), tab, tab],
        out_specs=(head_spec, head_spec,
                   pl.BlockSpec((1, MLA_HEADS * MLA_V, ts), lambda b, s: (b, 0, s))),
        compiler_params=_cparams("parallel", "parallel"),
        name="mla_proj",
    )(h, g1, wc, gq, gkv, wuq, wuk, wuv, cos, sin)


ATTN_ONES_ROWS = 16


def _attn_kernel(q_ref, k_ref, vt_ref, o_ref, m_scr, acc_scr, sa_scr, bma_scr, sb_scr, bmb_scr, *, tq):
    qi = pl.program_id(2)
    m_scr[...] = jnp.full(m_scr.shape, NEG_INF, F32)
    acc_scr[...] = jnp.zeros(acc_scr.shape, F32)
    ones_rows = jnp.ones((ATTN_ONES_ROWS, tq), BF16)

    bufs = ((sa_scr, bma_scr), (sb_scr, bmb_scr))

    def score_stage(blk, buf, masked):
        s_scr, bm_scr = bufs[buf]
        start = pl.multiple_of(blk * tq, tq)
        for hh in range(2):
            st = _dot_nt(k_ref[0, hh, pl.ds(start, tq), :], q_ref[0, hh])
            if masked:
                key = lax.broadcasted_iota(jnp.int32, st.shape, 0)
                qry = lax.broadcasted_iota(jnp.int32, st.shape, 1)
                st = jnp.where(key <= qry, st, NEG_INF)
            s_scr[hh] = st
            bm_scr[hh] = jnp.max(st, axis=0, keepdims=True)

    def value_stage(blk, buf):
        s_scr, bm_scr = bufs[buf]
        start = pl.multiple_of(blk * tq, tq)
        for hh in range(2):
            m_prev = m_scr[hh]
            m_new = jnp.maximum(m_prev, bm_scr[hh])
            alpha = jnp.exp2(m_prev - m_new)
            pt = jnp.exp2(s_scr[hh] - m_new).astype(BF16)
            vt = vt_ref[0, hh * MLA_V:(hh + 1) * MLA_V, pl.ds(start, tq)]
            vt = jnp.concatenate([vt, ones_rows], axis=0)
            acc_scr[hh] = alpha * acc_scr[hh] + _dot(vt, pt)
            m_scr[hh] = m_new

    score_stage(qi, 0, True)
    pairs = qi // 2

    def body(p, carry):
        score_stage(2 * p, 1, False)
        value_stage(jnp.where(p == 0, qi, 2 * p - 1), 0)
        score_stage(2 * p + 1, 0, False)
        value_stage(2 * p, 1)
        return carry

    lax.fori_loop(0, pairs, body, 0)
    last = jnp.where(qi < 2, qi, 2 * pairs - 1)

    @pl.when(qi % 2 == 0)
    def _():
        value_stage(last, 0)

    @pl.when(qi % 2 == 1)
    def _():
        score_stage(qi - 1, 1, False)
        value_stage(last, 0)
        value_stage(qi - 1, 1)

    ot = jnp.concatenate([acc_scr[hh, :MLA_V] / acc_scr[hh, MLA_V:MLA_V + 1] for hh in range(2)],
                         axis=0)
    o_ref[0] = ot.T.astype(BF16)


def _attention(q, k, vt, tq):
    B, H, S, _ = q.shape
    return pl.pallas_call(
        functools.partial(_attn_kernel, tq=tq),
        out_shape=jax.ShapeDtypeStruct((B, S, H * MLA_V), BF16),
        grid=(B, H // 2, S // tq),
        in_specs=[pl.BlockSpec((1, 2, tq, HEAD_PAD), lambda b, p, i: (b, p, i, 0)),
                  pl.BlockSpec((1, 2, S, HEAD_PAD), lambda b, p, i: (b, p, 0, 0)),
                  pl.BlockSpec((1, 2 * MLA_V, S), lambda b, p, i: (b, p, 0))],
        out_specs=pl.BlockSpec((1, tq, 2 * MLA_V), lambda b, p, i: (b, i, p)),
        scratch_shapes=[pltpu.VMEM((2, 1, tq), F32),
                        pltpu.VMEM((2, MLA_V + ATTN_ONES_ROWS, tq), F32),
                        pltpu.VMEM((2, tq, tq), F32), pltpu.VMEM((2, 1, tq), F32),
                        pltpu.VMEM((2, tq, tq), F32), pltpu.VMEM((2, 1, tq), F32)],
        compiler_params=_cparams("parallel", "parallel", "arbitrary"),
        name="mla_attention",
    )(q, k, vt)


def _mm_res_kernel(a_ref, w_ref, b_ref, h_ref, o_ref):
    o_ref[...] = h_ref[...] + _dot(a_ref[...].astype(BF16), w_ref[...]) + b_ref[...]


def _mm_res(a, w, bias, h, tm):
    T, K = a.shape
    N = w.shape[1]
    return pl.pallas_call(
        _mm_res_kernel,
        out_shape=jax.ShapeDtypeStruct((T, N), F32),
        grid=(T // tm,),
        in_specs=[pl.BlockSpec((tm, K), lambda i: (i, 0)), pl.BlockSpec((K, N), lambda i: (0, 0)),
                  pl.BlockSpec((1, N), lambda i: (0, 0)), pl.BlockSpec((tm, N), lambda i: (i, 0))],
        out_specs=pl.BlockSpec((tm, N), lambda i: (i, 0)),
        compiler_params=_cparams("parallel"),
        name="matmul_residual",
    )(a, w.astype(BF16), bias[None, :], h)


def _norm_mm_kernel(h_ref, g_ref, w_ref, o_ref, xn_scr):
    @pl.when(pl.program_id(1) == 0)
    def _():
        xn_scr[...] = _rms(h_ref[...], g_ref[...]).astype(BF16)

    o_ref[...] = _dot(xn_scr[...], w_ref[...])


def _norm_mm(h, g, w, tm, tn):
    T, K = h.shape
    N = w.shape[1]
    return pl.pallas_call(
        _norm_mm_kernel,
        out_shape=jax.ShapeDtypeStruct((T, N), F32),
        grid=(T // tm, N // tn),
        in_specs=[pl.BlockSpec((tm, K), lambda i, j: (i, 0)), pl.BlockSpec((1, K), lambda i, j: (0, 0)),
                  pl.BlockSpec((K, tn), lambda i, j: (0, j))],
        out_specs=pl.BlockSpec((tm, tn), lambda i, j: (i, j)),
        scratch_shapes=[pltpu.VMEM((tm, K), BF16)],
        compiler_params=_cparams("parallel", "arbitrary"),
        name="norm_matmul",
    )(h, g[None, :], w.astype(BF16))


def _ffn_kernel(h_ref, g_ref, wg_ref, wu_ref, wd_ref, o_ref, xn_scr, acc_scr):
    f = pl.program_id(1)

    @pl.when(f == 0)
    def _():
        xn_scr[...] = _rms(h_ref[...], g_ref[...]).astype(BF16)
        acc_scr[...] = h_ref[...]

    xn = xn_scr[...]
    gate = _dot(xn, wg_ref[...])
    up = _dot(xn, wu_ref[...])
    act = (gate * _sigmoid(gate) * up).astype(BF16)
    acc_scr[...] += _dot(act, wd_ref[...])

    @pl.when(f == pl.num_programs(1) - 1)
    def _():
        o_ref[...] = acc_scr[...]


def _ffn(h, g, w_gu, w_down, tm, tf):
    T, D = h.shape
    nf = D_FF // tf
    w_gu = w_gu.astype(BF16)
    return pl.pallas_call(
        _ffn_kernel,
        out_shape=jax.ShapeDtypeStruct((T, D), F32),
        grid=(T // tm, nf),
        in_specs=[pl.BlockSpec((tm, D), lambda i, f: (i, 0)), pl.BlockSpec((1, D), lambda i, f: (0, 0)),
                  pl.BlockSpec((D, tf), lambda i, f: (0, f)),
                  pl.BlockSpec((D, tf), lambda i, f: (0, f + nf)),
                  pl.BlockSpec((tf, D), lambda i, f: (f, 0))],
        out_specs=pl.BlockSpec((tm, D), lambda i, f: (i, 0)),
        scratch_shapes=[pltpu.VMEM((tm, D), BF16), pltpu.VMEM((tm, D), F32)],
        compiler_params=_cparams("parallel", "arbitrary"),
        name="swiglu_ffn",
    )(h, g[None, :], w_gu, w_gu, w_down.astype(BF16))


def _ple_kernel(h_ref, p_ref, g_ref, wg_ref, wp_ref, gf_ref, o_ref, *, final_norm):
    h = h_ref[...]
    gate = _sigmoid(_dot(_rms(h, g_ref[...]).astype(BF16), wg_ref[...]))
    out = h + _dot(p_ref[...].astype(BF16), wp_ref[...]) * gate
    if final_norm:
        out = _rms(out, gf_ref[...])
    o_ref[...] = out


def _ple(h, p, g, w_gate, w_proj, g_final, final_norm, tm):
    T, D = h.shape
    return pl.pallas_call(
        functools.partial(_ple_kernel, final_norm=final_norm),
        out_shape=jax.ShapeDtypeStruct((T, D), F32),
        grid=(T // tm,),
        in_specs=[pl.BlockSpec((tm, D), lambda i: (i, 0)), pl.BlockSpec((tm, PLE_DIM), lambda i: (i, 0)),
                  pl.BlockSpec((1, D), lambda i: (0, 0)), pl.BlockSpec((D, D), lambda i: (0, 0)),
                  pl.BlockSpec((PLE_DIM, D), lambda i: (0, 0)), pl.BlockSpec((1, D), lambda i: (0, 0))],
        out_specs=pl.BlockSpec((tm, D), lambda i: (i, 0)),
        compiler_params=_cparams("parallel"),
        name="per_layer_embedding",
    )(h, p, g[None, :], w_gate.astype(BF16), w_proj.astype(BF16), g_final[None, :])


def _glu_kernel(h_ref, g_ref, wa_ref, wb_ref, ba_ref, bb_ref, o_ref, xn_scr):
    @pl.when(pl.program_id(1) == 0)
    def _():
        xn_scr[...] = _rms(h_ref[...], g_ref[...]).astype(BF16)

    xn = xn_scr[...]
    a = _dot(xn, wa_ref[...]) + ba_ref[...]
    b = _dot(xn, wb_ref[...]) + bb_ref[...]
    o_ref[...] = a * _sigmoid(b)


def _conv_glu(h, g, w_pw1, b_pw1, tm, tn):
    T, D = h.shape
    nn = D // tn
    w = w_pw1.astype(BF16)
    b = b_pw1[None, :]
    return pl.pallas_call(
        _glu_kernel,
        out_shape=jax.ShapeDtypeStruct((T, D), F32),
        grid=(T // tm, nn),
        in_specs=[pl.BlockSpec((tm, D), lambda i, j: (i, 0)), pl.BlockSpec((1, D), lambda i, j: (0, 0)),
                  pl.BlockSpec((D, tn), lambda i, j: (0, j)), pl.BlockSpec((D, tn), lambda i, j: (0, j + nn)),
                  pl.BlockSpec((1, tn), lambda i, j: (0, j)), pl.BlockSpec((1, tn), lambda i, j: (0, j + nn))],
        out_specs=pl.BlockSpec((tm, tn), lambda i, j: (i, j)),
        scratch_shapes=[pltpu.VMEM((tm, D), BF16)],
        compiler_params=_cparams("parallel", "arbitrary"),
        name="conv_glu",
    )(h, g[None, :], w, w, b, b)


CONV_HALO = 32
CONV_ROWS = 64


def _conv_tail_kernel(a_ref, h_ref, wdw_ref, bdw_ref, lng_ref, lnb_ref, w2_ref, b2_ref, o_ref,
                      ext_scr, y_scr, *, ts):
    @pl.when(pl.program_id(1) == 0)
    def _():
        ext_scr[0:CONV_HALO, :] = jnp.zeros((CONV_HALO, D_MODEL), F32)

    ext_scr[CONV_HALO:, :] = a_ref[0]
    off = CONV_HALO - (CONV_WIDTH - 1)
    for r0 in range(0, ts, CONV_ROWS):
        for c0 in range(0, D_MODEL, LANES):
            acc = jnp.zeros((CONV_ROWS, LANES), F32)
            for j in range(CONV_WIDTH):
                acc = acc + wdw_ref[j:j + 1, c0:c0 + LANES] * ext_scr[r0 + off + j:r0 + off + j + CONV_ROWS, c0:c0 + LANES]
            y_scr[r0:r0 + CONV_ROWS, c0:c0 + LANES] = acc
    ext_scr[0:CONV_HALO, :] = ext_scr[ts:ts + CONV_HALO, :]
    y = y_scr[...] + bdw_ref[...]
    mu = jnp.mean(y, axis=-1, keepdims=True)
    yc = y - mu
    var = jnp.mean(yc * yc, axis=-1, keepdims=True)
    z = yc * lax.rsqrt(var + EPS) * lng_ref[...] + lnb_ref[...]
    z = (z * _sigmoid(z)).astype(BF16)
    o_ref[0] = h_ref[0] + _dot(z, w2_ref[...]) + b2_ref[...]


def _conv_tail(a, h, w_dw, b_dw, ln_g, ln_b, w_pw2, b_pw2, ts):
    B, S, D = h.shape

    def full(x):
        return pl.BlockSpec(x.shape, lambda b, s: (0,) * x.ndim)

    tile = pl.BlockSpec((1, ts, D), lambda b, s: (b, s, 0))
    args = (w_dw, b_dw[None, :], ln_g[None, :], ln_b[None, :], w_pw2.astype(BF16), b_pw2[None, :])
    return pl.pallas_call(
        functools.partial(_conv_tail_kernel, ts=ts),
        out_shape=jax.ShapeDtypeStruct((B, S, D), F32),
        grid=(B, S // ts),
        in_specs=[tile, tile] + [full(x) for x in args],
        out_specs=tile,
        scratch_shapes=[pltpu.VMEM((ts + CONV_HALO, D), F32), pltpu.VMEM((ts, D), F32)],
        compiler_params=_cparams("parallel", "arbitrary"),
        name="conv_tail",
    )(a, h, *args)


def _lower_bounds_kernel(x_ref, o_ref):
    x = x_ref[...]
    n = x.shape[0]
    m = x[0:1]
    for r in range(1, n):
        m = jnp.maximum(m, x[r:r + 1])
    e = jnp.exp(x - m)
    tot = e[0:1]
    for r in range(1, n):
        tot = tot + e[r:r + 1]
    sm = e / tot
    cum = sm[0:1]
    first = cum
    o_ref[0:1, :] = cum - first
    for r in range(1, n):
        cum = cum + sm[r:r + 1]
        o_ref[r:r + 1, :] = cum - first


def _lower_bounds(logits):
    return pl.pallas_call(
        _lower_bounds_kernel,
        out_shape=jax.ShapeDtypeStruct(logits.shape, F32),
        name="hgrn_lower_bounds",
    )(logits)


def _cumsum_rows(x):
    row = lax.broadcasted_iota(jnp.int32, x.shape, 0)
    step = 1
    while step < x.shape[0]:
        x = x + jnp.where(row >= step, pltpu.roll(x, step, axis=0), 0.0)
        step *= 2
    return x


def _bcast_rows(b, idx):
    return jnp.concatenate([jnp.broadcast_to(b[i:i + 1, :], (8, b.shape[1])) for i in idx], axis=0)


def _hgrn_chunk(q, fz, v, lb, st):
    C = CHUNK
    log_sig = jnp.minimum(fz, 0.0) - jnp.log1p(jnp.exp(-jnp.abs(fz)))
    t1 = jnp.log(lb)
    t2 = jnp.log1p(-lb) + log_sig
    hi = jnp.maximum(t1, t2)
    lf = hi + jnp.log1p(jnp.exp(-jnp.abs(t1 - t2)))
    kk = 1.0 - jnp.exp(lf)
    b = _cumsum_rows(lf)
    b_last = b[C - 1:C, :]
    row = lax.broadcasted_iota(jnp.int32, (C, LANES), 0)

    o = _dot_nt((q * jnp.exp(b)).astype(BF16), st.astype(BF16))
    kd = (kk * jnp.exp(b_last - b)).astype(BF16)
    st_new = st * jnp.exp(b_last) + _dot_tn(v.astype(BF16), kd)

    trow = lax.broadcasted_iota(jnp.int32, (C, C), 0)
    scol = lax.broadcasted_iota(jnp.int32, (C, C), 1)
    attn = jnp.zeros((C, C), F32)
    for m in (32, 16, 8):
        sh = m.bit_length() - 1
        odd = ((row >> sh) & 1) == 1
        ref_idx = []
        for g8 in range(C // 8):
            blk = (g8 * 8) // m
            ref_idx.append(blk * m - 1 if blk % 2 == 1 else (blk + 1) * m - 1)
        ref = _bcast_rows(b, ref_idx)
        e = jnp.exp(jnp.where(odd, b - ref, ref - b))
        qs = jnp.where(odd, q * e, 0.0).astype(BF16)
        ks = jnp.where(odd, 0.0, kk * e).astype(BF16)
        a_m = _dot_nt(qs, ks)
        pair = (((trow >> sh) & 1) == 1) & ((scol >> sh) == (trow >> sh) - 1)
        attn = attn + jnp.where(pair, a_m, 0.0)

    for d in range(8):
        if d == 0:
            w = q * kk
        else:
            valid = (row & 7) >= d
            diff = jnp.where(valid, b - pltpu.roll(b, d, axis=0), NEG_INF)
            w = q * pltpu.roll(kk, d, axis=0) * jnp.exp(diff)
        val = jnp.sum(w, axis=1, keepdims=True)
        attn = attn + jnp.where(trow - scol == d, val, 0.0)

    o = o + _dot(attn.astype(BF16), v.astype(BF16))
    return o, st_new


def _hgrn_scan_kernel(q_ref, fz_ref, v_ref, gate_ref, lb_ref, ng_ref, o_ref, st_scr, *, ts):
    @pl.when(pl.program_id(2) == 0)
    def _():
        st_scr[...] = jnp.zeros(st_scr.shape, F32)

    lb = lb_ref[...]
    ng = ng_ref[...]

    def body(c, carry):
        r0 = pl.multiple_of(c * CHUNK, CHUNK)
        rows = pl.ds(r0, CHUNK)
        o, st_new = _hgrn_chunk(q_ref[0, rows, :], fz_ref[0, rows, :], v_ref[0, rows, :], lb, st_scr[...])
        st_scr[...] = st_new
        o = _rms(o, ng) * _sigmoid(gate_ref[0, rows, :])
        o_ref[0, rows, :] = o.astype(BF16)
        return carry

    lax.fori_loop(0, ts // CHUNK, body, 0)


def _hgrn_scan(proj, lb, norm_g, ts):
    B, S, _ = proj.shape
    H = HGRN_HEADS

    def col(off):
        return pl.BlockSpec((1, ts, HGRN_HEAD_DIM), lambda b, h, s: (b, s, h + off))

    vec = pl.BlockSpec((1, HGRN_HEAD_DIM), lambda b, h, s: (0, h))
    return pl.pallas_call(
        functools.partial(_hgrn_scan_kernel, ts=ts),
        out_shape=jax.ShapeDtypeStruct((B, S, H * HGRN_HEAD_DIM), BF16),
        grid=(B, H, S // ts),
        in_specs=[col(0), col(H), col(2 * H), col(3 * H), vec, vec],
        out_specs=pl.BlockSpec((1, ts, HGRN_HEAD_DIM), lambda b, h, s: (b, s, h)),
        scratch_shapes=[pltpu.VMEM((HGRN_HEAD_DIM, HGRN_HEAD_DIM), F32)],
        compiler_params=_cparams("parallel", "parallel", "arbitrary"),
        name="hgrn_scan",
    )(proj, proj, proj, proj, lb, norm_g[None, :])


def _tile(n, want):
    t = min(n, want)
    assert n % t == 0, (n, want)
    return t


def kernel(x, p, positions, norm1_g, norm2_g, mla_w_in, mla_q_norm_g, mla_w_uq, mla_kv_norm_g, mla_w_ukv, mla_w_out, conv_w_pw1, conv_b_pw1, conv_w_dw, conv_b_dw, conv_ln_g, conv_ln_b, conv_w_pw2, conv_b_pw2, hgrn_w_in, hgrn_lb_logits, hgrn_norm_g, hgrn_w_out, ffn_w_gu, ffn_w_down, ple_w_proj, ple_norm_g, ple_w_gate, final_norm_g):
    B, S, D = x.shape
    depth = norm1_g.shape[0]
    T = B * S
    tm = _tile(T, 512)
    ts_proj = _tile(S, 256)
    tq = _tile(S, 512)
    ts_conv = _tile(S, 256)
    ts_scan = _tile(S, 512)
    zero_bias = jnp.zeros((D,), F32)

    lower_bounds = _lower_bounds(hgrn_lb_logits.astype(F32))
    cos, sin = _rope_tables(positions, _tile(S, 512))

    h = x
    for i in range(depth):
        mixer = i % N_MIXERS
        j = i // N_MIXERS
        if mixer == 0:
            q, k, v = _mla_proj(h, norm1_g[i], mla_w_in[j], mla_q_norm_g[j], mla_w_uq[j],
                                mla_kv_norm_g[j], mla_w_ukv[j], cos, sin, ts_proj)
            o = _attention(q, k, v, tq)
            h = _mm_res(o.reshape(T, D), mla_w_out[j], zero_bias, h.reshape(T, D), tm)
        elif mixer == 1:
            a = _conv_glu(h.reshape(T, D), norm1_g[i], conv_w_pw1[j], conv_b_pw1[j], tm, 512)
            h = _conv_tail(a.reshape(B, S, D), h.reshape(B, S, D), conv_w_dw[j], conv_b_dw[j],
                           conv_ln_g[j], conv_ln_b[j], conv_w_pw2[j], conv_b_pw2[j], ts_conv)
        else:
            proj = _norm_mm(h.reshape(T, D), norm1_g[i], hgrn_w_in[j], tm, 1024)
            o = _hgrn_scan(proj.reshape(B, S, 4 * D), lower_bounds[i:i + 1], hgrn_norm_g[j], ts_scan)
            h = _mm_res(o.reshape(T, D), hgrn_w_out[j], zero_bias, h.reshape(T, D), tm)
        h = _ffn(h.reshape(T, D), norm2_g[i], ffn_w_gu[i], ffn_w_down[i], tm, D_FF // 2)
        h = _ple(h, p[i].reshape(T, PLE_DIM), ple_norm_g[i], ple_w_gate[i], ple_w_proj[i],
                 final_norm_g, i == depth - 1, tm)
        h = h.reshape(B, S, D)
    return h
```

```python
import functools
import math

import jax
import jax.numpy as jnp
from jax import lax
from jax.experimental import pallas as pl
from jax.experimental.pallas import tpu as pltpu

D_MODEL = 1024
N_MIXERS = 3
PLE_DIM = 256
D_FF = 2816
MLA_HEADS = 16
MLA_NOPE = 64
MLA_ROPE = 32
MLA_V = 64
MLA_Q_LORA = 384
MLA_KV_LORA = 256
ROPE_BASE = 10000.0
CONV_WIDTH = 31
HGRN_HEADS = 8
HGRN_HEAD_DIM = 128
CHUNK = 64
EPS = 1e-6

LANES = 128
SUBLANES = 8
HEAD_PAD = 128
LOG2E = 1.4426950408889634
F32 = jnp.float32
BF16 = jnp.bfloat16
NEG_INF = float("-inf")
VMEM_LIMIT = 56 * 1024 * 1024


def _cparams(*sem):
    return pltpu.CompilerParams(dimension_semantics=sem, vmem_limit_bytes=VMEM_LIMIT)


def _rms(x, g):
    return x * lax.rsqrt(jnp.mean(x * x, axis=-1, keepdims=True) + EPS) * g


def _dot(a, b):
    return jnp.dot(a, b, preferred_element_type=F32)


def _dot_nt(a, b):
    return lax.dot_general(a, b, (((1,), (1,)), ((), ())), preferred_element_type=F32)


def _dot_tn(a, b):
    return lax.dot_general(a, b, (((0,), (0,)), ((), ())), preferred_element_type=F32)


def _sigmoid(x):
    return 1.0 / (1.0 + jnp.exp(-x))


def _rope_table_kernel(pos_ref, invf_ref, sign_ref, cos_ref, sin_ref):
    ang = pos_ref[0].astype(F32) * invf_ref[...]
    cos_ref[0] = jnp.cos(ang)
    sin_ref[0] = jnp.sin(ang) * sign_ref[...]


def _rope_tables(positions, ts):
    B, S = positions.shape
    inv_freq = 1.0 / (ROPE_BASE ** (jnp.arange(0, MLA_ROPE, 2, dtype=F32) / MLA_ROPE))
    half = MLA_ROPE // 2
    zeros_n = jnp.zeros((MLA_NOPE,), F32)
    zeros_p = jnp.zeros((HEAD_PAD - MLA_NOPE - MLA_ROPE,), F32)
    invf = jnp.concatenate([zeros_n, inv_freq, inv_freq, zeros_p])[None, :]
    sign = jnp.concatenate([zeros_n, -jnp.ones((half,), F32), jnp.ones((half,), F32), zeros_p])[None, :]
    vec = pl.BlockSpec((1, HEAD_PAD), lambda b, s: (0, 0))
    tab = pl.BlockSpec((1, ts, HEAD_PAD), lambda b, s: (b, s, 0))
    return pl.pallas_call(
        _rope_table_kernel,
        out_shape=(jax.ShapeDtypeStruct((B, S, HEAD_PAD), F32),) * 2,
        grid=(B, S // ts),
        in_specs=[pl.BlockSpec((1, ts, 1), lambda b, s: (b, s, 0)), vec, vec],
        out_specs=(tab, tab),
        compiler_params=_cparams("parallel", "parallel"),
        name="rope_tables",
    )(positions[:, :, None], invf, sign)


def _rope_apply(x, cos, sin_signed, lane):
    half = MLA_ROPE // 2
    from_hi = pltpu.roll(x, HEAD_PAD - half, axis=1)
    from_lo = pltpu.roll(x, half, axis=1)
    swapped = jnp.where(lane < MLA_NOPE + half, from_hi, from_lo)
    return x * cos + swapped * sin_signed


def _mla_proj_kernel(h_ref, g1_ref, wc_ref, gq_ref, gkv_ref, wuq_ref, wuk_ref, wuv_ref,
                     cos_ref, sin_ref, q_ref, k_ref, v_ref, *, q_scale):
    xn = _rms(h_ref[0], g1_ref[...]).astype(BF16)
    c = _dot(xn, wc_ref[...])
    cq = c[:, :MLA_Q_LORA]
    ckv = c[:, MLA_Q_LORA:MLA_Q_LORA + MLA_KV_LORA]
    kr = c[:, MLA_Q_LORA + MLA_KV_LORA:]
    cqn = _rms(cq, gq_ref[...]).astype(BF16)
    ckvn = _rms(ckv, gkv_ref[...]).astype(BF16)
    cos = cos_ref[0]
    sin = sin_ref[0]
    lane = lax.broadcasted_iota(jnp.int32, cos.shape, 1)
    kr_rot = _rope_apply(kr, cos, sin, lane)
    qf = _dot(cqn, wuq_ref[...])
    kf = _dot(ckvn, wuk_ref[...])
    for hh in range(MLA_HEADS):
        sl = slice(hh * HEAD_PAD, (hh + 1) * HEAD_PAD)
        q_ref[0, hh] = (_rope_apply(qf[:, sl], cos, sin, lane) * q_scale).astype(BF16)
        k_ref[0, hh] = (kf[:, sl] + kr_rot).astype(BF16)
    v_ref[0] = _dot_nt(wuv_ref[...], ckvn).astype(BF16)


def _mla_proj(h, g1, w_in, gq, w_uq, gkv, w_ukv, cos, sin, ts):
    B, S, D = h.shape
    nq, nkv = MLA_Q_LORA, MLA_KV_LORA
    pad = HEAD_PAD - MLA_NOPE - MLA_ROPE
    w_kr = jnp.pad(w_in[:, nq + nkv:], ((0, 0), (MLA_NOPE, pad)))
    wc = jnp.concatenate([w_in[:, :nq + nkv], w_kr], axis=1).astype(BF16)
    wuq = jnp.pad(w_uq.reshape(nq, MLA_HEADS, MLA_NOPE + MLA_ROPE), ((0, 0), (0, 0), (0, pad)))
    wuq = wuq.reshape(nq, MLA_HEADS * HEAD_PAD).astype(BF16)
    wukv = w_ukv.reshape(nkv, MLA_HEADS, MLA_NOPE + MLA_V)
    wuk = jnp.pad(wukv[:, :, :MLA_NOPE], ((0, 0), (0, 0), (0, HEAD_PAD - MLA_NOPE)))
    wuk = wuk.reshape(nkv, MLA_HEADS * HEAD_PAD).astype(BF16)
    wuv = wukv[:, :, MLA_NOPE:].reshape(nkv, MLA_HEADS * MLA_V).T.astype(BF16)
    q_scale = (MLA_NOPE + MLA_ROPE) ** -0.5 * LOG2E

    def full(a):
        return pl.BlockSpec(a.shape, lambda b, s: (0,) * a.ndim)

    g1, gq, gkv = g1[None, :], gq[None, :], gkv[None, :]
    tab = pl.BlockSpec((1, ts, HEAD_PAD), lambda b, s: (b, s, 0))
    head_spec = pl.BlockSpec((1, MLA_HEADS, ts, HEAD_PAD), lambda b, s: (b, 0, s, 0))
    qk_shape = jax.ShapeDtypeStruct((B, MLA_HEADS, S, HEAD_PAD), BF16)
    return pl.pallas_call(
        functools.partial(_mla_proj_kernel, q_scale=q_scale),
        out_shape=(qk_shape, qk_shape, jax.ShapeDtypeStruct((B, MLA_HEADS * MLA_V, S), BF16)),
        grid=(B, S // ts),
        in_specs=[pl.BlockSpec((1, ts, D), lambda b, s: (b, s, 0)), full(g1), full(wc), full(gq),
                  full(gkv), full(wuq), full(wuk), full(wuv), tab, tab],
        out_specs=(head_spec, head_spec,
                   pl.BlockSpec((1, MLA_HEADS * MLA_V, ts), lambda b, s: (b, 0, s))),
        compiler_params=_cparams("parallel", "parallel"),
        name="mla_proj",
    )(h, g1, wc, gq, gkv, wuq, wuk, wuv, cos, sin)


ATTN_ONES_ROWS = 16


def _attn_kernel(q_ref, k_ref, vt_ref, o_ref, m_scr, acc_scr, sa_scr, bma_scr, sb_scr, bmb_scr, *, tq):
    qi = pl.program_id(2)
    m_scr[...] = jnp.full(m_scr.shape, NEG_INF, F32)
    acc_scr[...] = jnp.zeros(acc_scr.shape, F32)
    ones_rows = jnp.ones((ATTN_ONES_ROWS, tq), BF16)

    bufs = ((sa_scr, bma_scr), (sb_scr, bmb_scr))

    def score_stage(blk, buf, masked):
        s_scr, bm_scr = bufs[buf]
        start = pl.multiple_of(blk * tq, tq)
        for hh in range(2):
            st = _dot_nt(k_ref[0, hh, pl.ds(start, tq), :], q_ref[0, hh])
            if masked:
                key = lax.broadcasted_iota(jnp.int32, st.shape, 0)
                qry = lax.broadcasted_iota(jnp.int32, st.shape, 1)
                st = jnp.where(key <= qry, st, NEG_INF)
            s_scr[hh] = st
            bm_scr[hh] = jnp.max(st, axis=0, keepdims=True)

    def value_stage(blk, buf):
        s_scr, bm_scr = bufs[buf]
        start = pl.multiple_of(blk * tq, tq)
        for hh in range(2):
            m_prev = m_scr[hh]
            m_new = jnp.maximum(m_prev, bm_scr[hh])
            alpha = jnp.exp2(m_prev - m_new)
            pt = jnp.exp2(s_scr[hh] - m_new).astype(BF16)
            vt = vt_ref[0, hh * MLA_V:(hh + 1) * MLA_V, pl.ds(start, tq)]
            vt = jnp.concatenate([vt, ones_rows], axis=0)
            acc_scr[hh] = alpha * acc_scr[hh] + _dot(vt, pt)
            m_scr[hh] = m_new

    score_stage(qi, 0, True)
    pairs = qi // 2

    def body(p, carry):
        score_stage(2 * p, 1, False)
        value_stage(jnp.where(p == 0, qi, 2 * p - 1), 0)
        score_stage(2 * p + 1, 0, False)
        value_stage(2 * p, 1)
        return carry

    lax.fori_loop(0, pairs, body, 0)
    last = jnp.where(qi < 2, qi, 2 * pairs - 1)

    @pl.when(qi % 2 == 0)
    def _():
        value_stage(last, 0)

    @pl.when(qi % 2 == 1)
    def _():
        score_stage(qi - 1, 1, False)
        value_stage(last, 0)
        value_stage(qi - 1, 1)

    ot = jnp.concatenate([acc_scr[hh, :MLA_V] / acc_scr[hh, MLA_V:MLA_V + 1] for hh in range(2)],
                         axis=0)
    o_ref[0] = ot.T.astype(BF16)


def _attention(q, k, vt, tq):
    B, H, S, _ = q.shape
    return pl.pallas_call(
        functools.partial(_attn_kernel, tq=tq),
        out_shape=jax.ShapeDtypeStruct((B, S, H * MLA_V), BF16),
        grid=(B, H // 2, S // tq),
        in_specs=[pl.BlockSpec((1, 2, tq, HEAD_PAD), lambda b, p, i: (b, p, i, 0)),
                  pl.BlockSpec((1, 2, S, HEAD_PAD), lambda b, p, i: (b, p, 0, 0)),
                  pl.BlockSpec((1, 2 * MLA_V, S), lambda b, p, i: (b, p, 0))],
        out_specs=pl.BlockSpec((1, tq, 2 * MLA_V), lambda b, p, i: (b, i, p)),
        scratch_shapes=[pltpu.VMEM((2, 1, tq), F32),
                        pltpu.VMEM((2, MLA_V + ATTN_ONES_ROWS, tq), F32),
                        pltpu.VMEM((2, tq, tq), F32), pltpu.VMEM((2, 1, tq), F32),
                        pltpu.VMEM((2, tq, tq), F32), pltpu.VMEM((2, 1, tq), F32)],
        compiler_params=_cparams("parallel", "parallel", "arbitrary"),
        name="mla_attention",
    )(q, k, vt)


def _norm_mm_kernel(h_ref, g_ref, w_ref, o_ref, xn_scr):
    @pl.when(pl.program_id(1) == 0)
    def _():
        xn_scr[...] = _rms(h_ref[...], g_ref[...]).astype(BF16)

    o_ref[...] = _dot(xn_scr[...], w_ref[...])


def _norm_mm(h, g, w, tm, tn):
    T, K = h.shape
    N = w.shape[1]
    return pl.pallas_call(
        _norm_mm_kernel,
        out_shape=jax.ShapeDtypeStruct((T, N), F32),
        grid=(T // tm, N // tn),
        in_specs=[pl.BlockSpec((tm, K), lambda i, j: (i, 0)), pl.BlockSpec((1, K), lambda i, j: (0, 0)),
                  pl.BlockSpec((K, tn), lambda i, j: (0, j))],
        out_specs=pl.BlockSpec((tm, tn), lambda i, j: (i, j)),
        scratch_shapes=[pltpu.VMEM((tm, K), BF16)],
        compiler_params=_cparams("parallel", "arbitrary"),
        name="norm_matmul",
    )(h, g[None, :], w.astype(BF16))


def _tail_kernel(h_ref, a_ref, wo_ref, bo_ref, g2_ref, wg_ref, wu_ref, wd_ref, p_ref, gp_ref,
                 wgate_ref, wproj_ref, gf_ref, o_ref, xn_scr, acc_scr, *, final_norm):
    f = pl.program_id(1)

    @pl.when(f == 0)
    def _():
        h1 = h_ref[...] + _dot(a_ref[...], wo_ref[...]) + bo_ref[...]
        acc_scr[...] = h1
        xn_scr[...] = _rms(h1, g2_ref[...]).astype(BF16)

    xn = xn_scr[...]
    gate = _dot(xn, wg_ref[...])
    up = _dot(xn, wu_ref[...])
    act = (gate * _sigmoid(gate) * up).astype(BF16)
    acc_scr[...] += _dot(act, wd_ref[...])

    @pl.when(f == pl.num_programs(1) - 1)
    def _():
        h2 = acc_scr[...]
        pgate = _sigmoid(_dot(_rms(h2, gp_ref[...]).astype(BF16), wgate_ref[...]))
        out = h2 + _dot(p_ref[...].astype(BF16), wproj_ref[...]) * pgate
        if final_norm:
            out = _rms(out, gf_ref[...])
        o_ref[...] = out


def _layer_tail(h, a, w_o, b_o, g2, w_gu, w_down, p, g_ple, w_gate, w_proj, g_final, final_norm, tm, tf):
    T, D = h.shape
    nf = D_FF // tf
    w_gu = w_gu.astype(BF16)

    def tile(w):
        return pl.BlockSpec((tm, w), lambda i, f: (i, 0))

    def resident(shape):
        return pl.BlockSpec(shape, lambda i, f: (0, 0), pipeline_mode=pl.Buffered(1))

    return pl.pallas_call(
        functools.partial(_tail_kernel, final_norm=final_norm),
        out_shape=jax.ShapeDtypeStruct((T, D), F32),
        grid=(T // tm, nf),
        in_specs=[tile(D), tile(D), resident((D, D)), resident((1, D)), resident((1, D)),
                  pl.BlockSpec((D, tf), lambda i, f: (0, f)),
                  pl.BlockSpec((D, tf), lambda i, f: (0, f + nf)),
                  pl.BlockSpec((tf, D), lambda i, f: (f, 0)),
                  tile(PLE_DIM), resident((1, D)), resident((D, D)), resident((PLE_DIM, D)),
                  resident((1, D))],
        out_specs=tile(D),
        scratch_shapes=[pltpu.VMEM((tm, D), BF16), pltpu.VMEM((tm, D), F32)],
        compiler_params=_cparams("parallel", "arbitrary"),
        name="layer_tail",
    )(h, a, w_o.astype(BF16), b_o[None, :], g2[None, :], w_gu, w_gu, w_down.astype(BF16), p,
      g_ple[None, :], w_gate.astype(BF16), w_proj.astype(BF16), g_final[None, :])


def _glu_kernel(h_ref, g_ref, wa_ref, wb_ref, ba_ref, bb_ref, o_ref, xn_scr):
    @pl.when(pl.program_id(1) == 0)
    def _():
        xn_scr[...] = _rms(h_ref[...], g_ref[...]).astype(BF16)

    xn = xn_scr[...]
    a = _dot(xn, wa_ref[...]) + ba_ref[...]
    b = _dot(xn, wb_ref[...]) + bb_ref[...]
    o_ref[...] = a * _sigmoid(b)


def _conv_glu(h, g, w_pw1, b_pw1, tm, tn):
    T, D = h.shape
    nn = D // tn
    w = w_pw1.astype(BF16)
    b = b_pw1[None, :]
    return pl.pallas_call(
        _glu_kernel,
        out_shape=jax.ShapeDtypeStruct((T, D), F32),
        grid=(T // tm, nn),
        in_specs=[pl.BlockSpec((tm, D), lambda i, j: (i, 0)), pl.BlockSpec((1, D), lambda i, j: (0, 0)),
                  pl.BlockSpec((D, tn), lambda i, j: (0, j)), pl.BlockSpec((D, tn), lambda i, j: (0, j + nn)),
                  pl.BlockSpec((1, tn), lambda i, j: (0, j)), pl.BlockSpec((1, tn), lambda i, j: (0, j + nn))],
        out_specs=pl.BlockSpec((tm, tn), lambda i, j: (i, j)),
        scratch_shapes=[pltpu.VMEM((tm, D), BF16)],
        compiler_params=_cparams("parallel", "arbitrary"),
        name="conv_glu",
    )(h, g[None, :], w, w, b, b)


CONV_HALO = 32
CONV_ROWS = 64


def _conv_dw_kernel(a_ref, wdw_ref, bdw_ref, lng_ref, lnb_ref, z_ref, ext_scr, y_scr, sh_scr, *, ts):
    @pl.when(pl.program_id(1) == 0)
    def _():
        ext_scr[0:CONV_HALO, :] = jnp.zeros((CONV_HALO, D_MODEL), F32)

    ext_scr[CONV_HALO:, :] = a_ref[0]
    off = CONV_HALO - (CONV_WIDTH - 1)
    rows_sh = sh_scr.shape[1]
    for ph in range(1, SUBLANES):
        sh_scr[ph - 1] = ext_scr[ph:ph + rows_sh, :]
    for r0 in range(0, ts, CONV_ROWS):
        for c0 in range(0, D_MODEL, LANES):
            acc = jnp.zeros((CONV_ROWS, LANES), F32)
            for j in range(CONV_WIDTH):
                grp, ph = divmod(off + j, SUBLANES)
                src = ext_scr if ph == 0 else sh_scr.at[ph - 1]
                start = r0 + grp * SUBLANES
                acc = acc + wdw_ref[j:j + 1, c0:c0 + LANES] * src[start:start + CONV_ROWS, c0:c0 + LANES]
            y_scr[r0:r0 + CONV_ROWS, c0:c0 + LANES] = acc
    ext_scr[0:CONV_HALO, :] = ext_scr[ts:ts + CONV_HALO, :]
    y = y_scr[...] + bdw_ref[...]
    mu = jnp.mean(y, axis=-1, keepdims=True)
    yc = y - mu
    var = jnp.mean(yc * yc, axis=-1, keepdims=True)
    z = yc * lax.rsqrt(var + EPS) * lng_ref[...] + lnb_ref[...]
    z_ref[0] = (z * _sigmoid(z)).astype(BF16)


def _conv_dw(a, w_dw, b_dw, ln_g, ln_b, ts):
    B, S, D = a.shape

    def full(x):
        return pl.BlockSpec(x.shape, lambda b, s: (0,) * x.ndim)

    tile = pl.BlockSpec((1, ts, D), lambda b, s: (b, s, 0))
    args = (w_dw, b_dw[None, :], ln_g[None, :], ln_b[None, :])
    return pl.pallas_call(
        functools.partial(_conv_dw_kernel, ts=ts),
        out_shape=jax.ShapeDtypeStruct((B, S, D), BF16),
        grid=(B, S // ts),
        in_specs=[tile] + [full(x) for x in args],
        out_specs=tile,
        scratch_shapes=[pltpu.VMEM((ts + CONV_HALO, D), F32), pltpu.VMEM((ts, D), F32),
                        pltpu.VMEM((SUBLANES - 1, ts + CONV_HALO - SUBLANES, D), F32)],
        compiler_params=_cparams("parallel", "arbitrary"),
        name="conv_depthwise",
    )(a, *args)


def _lower_bounds_kernel(x_ref, o_ref):
    x = x_ref[...]
    n = x.shape[0]
    m = x[0:1]
    for r in range(1, n):
        m = jnp.maximum(m, x[r:r + 1])
    e = jnp.exp(x - m)
    tot = e[0:1]
    for r in range(1, n):
        tot = tot + e[r:r + 1]
    sm = e / tot
    cum = sm[0:1]
    first = cum
    o_ref[0:1, :] = cum - first
    for r in range(1, n):
        cum = cum + sm[r:r + 1]
        o_ref[r:r + 1, :] = cum - first


def _lower_bounds(logits):
    return pl.pallas_call(
        _lower_bounds_kernel,
        out_shape=jax.ShapeDtypeStruct(logits.shape, F32),
        name="hgrn_lower_bounds",
    )(logits)


SUB = SUBLANES
NGRP = CHUNK // SUB
HGRN_LEVELS = (32, 16, 8)
HGRN_PACK = 2


def _groups(x):
    return [x[g * SUB:(g + 1) * SUB, :] for g in range(NGRP)]


def _cumsum_chunk(x, row):
    gs = _groups(x)
    for step in (1, 2, 4):
        gs = [g + jnp.where(row >= step, pltpu.roll(g, step, axis=0), 0.0) for g in gs]
    out = [gs[0]]
    run = gs[0][SUB - 1:SUB, :]
    for g in range(1, NGRP):
        out.append(gs[g] + run)
        run = run + gs[g][SUB - 1:SUB, :]
    return jnp.concatenate(out, axis=0)


def _level_operands(m, bg, qg, kg):
    zeros = jnp.zeros((SUB, LANES), F32)
    qs, ks = [], []
    for g in range(NGRP):
        blk = (g * SUB) // m
        if blk % 2 == 1:
            ref = bg[blk * m // SUB - 1][SUB - 1:SUB, :]
            qs.append(qg[g] * jnp.exp2(bg[g] - ref))
            ks.append(zeros)
        else:
            ref = bg[(blk + 1) * m // SUB - 1][SUB - 1:SUB, :]
            qs.append(zeros)
            ks.append(kg[g] * jnp.exp2(ref - bg[g]))
    return jnp.concatenate(qs, axis=0).astype(BF16), jnp.concatenate(ks, axis=0).astype(BF16)


def _hgrn_chunk(qs, fzs, vs, gates, consts, sts, masks):
    C = CHUNK
    row, pair_masks, diag_masks = masks
    heads = range(len(qs))
    lfs = []
    for h in heads:
        t1, l1p, _ = consts[h]
        log_sig = jnp.minimum(fzs[h], 0.0) - jnp.log(1.0 + jnp.exp(-jnp.abs(fzs[h])))
        t2 = l1p + log_sig
        lfs.append(jnp.maximum(t1, t2) + jnp.log(1.0 + jnp.exp(-jnp.abs(t1 - t2))))
    lfs = [lf * LOG2E for lf in lfs]
    kks = [1.0 - jnp.exp2(lfs[h]) for h in heads]
    bs = [_cumsum_chunk(lfs[h], row) for h in heads]
    b_lasts = [bs[h][C - 1:C, :] for h in heads]
    vbs = [vs[h].astype(BF16) for h in heads]

    os_ = [_dot_nt((qs[h] * jnp.exp2(bs[h])).astype(BF16), sts[h].astype(BF16)) for h in heads]
    kds = [(kks[h] * jnp.exp2(b_lasts[h] - bs[h])).astype(BF16) for h in heads]
    st_news = [sts[h] * jnp.exp2(b_lasts[h]) + _dot_tn(vbs[h], kds[h]) for h in heads]

    bgs = [_groups(bs[h]) for h in heads]
    qgs = [_groups(qs[h]) for h in heads]
    kgs = [_groups(kks[h]) for h in heads]
    attns = [jnp.zeros((C, C), F32) for _ in heads]
    for m, pair in zip(HGRN_LEVELS, pair_masks):
        ops = [_level_operands(m, bgs[h], qgs[h], kgs[h]) for h in heads]
        for h in heads:
            attns[h] = attns[h] + jnp.where(pair, _dot_nt(*ops[h]), 0.0)

    for d in range(SUB):
        for h in heads:
            if d == 0:
                w = qs[h] * kks[h]
            else:
                valid = row >= d
                w = jnp.concatenate(
                    [qgs[h][g] * pltpu.roll(kgs[h][g], d, axis=0)
                     * jnp.exp2(jnp.where(valid, bgs[h][g] - pltpu.roll(bgs[h][g], d, axis=0), NEG_INF))
                     for g in range(NGRP)], axis=0)
            val = jnp.sum(w, axis=1, keepdims=True)
            attns[h] = attns[h] + jnp.where(diag_masks[d], val, 0.0)

    outs = []
    for h in heads:
        o = os_[h] + _dot(attns[h].astype(BF16), vbs[h])
        outs.append(_rms(o, consts[h][2]) * _sigmoid(gates[h]))
    return outs, st_news


def _hgrn_scan_kernel(q_ref, fz_ref, v_ref, gate_ref, lb_ref, ng_ref, o_ref, st_scr, *, ts):
    @pl.when(pl.program_id(2) == 0)
    def _():
        st_scr[...] = jnp.zeros(st_scr.shape, F32)

    C = CHUNK
    hd = HGRN_HEAD_DIM
    lanes = [slice(h * hd, (h + 1) * hd) for h in range(HGRN_PACK)]
    consts = []
    for sl in lanes:
        lb = lb_ref[:, sl]
        consts.append((jnp.log(lb), jnp.log1p(-lb), ng_ref[:, sl]))
    row = lax.broadcasted_iota(jnp.int32, (SUB, LANES), 0)
    trow = lax.broadcasted_iota(jnp.int32, (C, C), 0)
    scol = lax.broadcasted_iota(jnp.int32, (C, C), 1)
    pair_masks = []
    for m in HGRN_LEVELS:
        sh = m.bit_length() - 1
        pair_masks.append((((trow >> sh) & 1) == 1) & ((scol >> sh) == (trow >> sh) - 1))
    diag_masks = [trow - scol == d for d in range(SUB)]
    masks = (row, pair_masks, diag_masks)

    def body(c, carry):
        r0 = pl.multiple_of(c * C, C)
        rows = pl.ds(r0, C)
        outs, st_news = _hgrn_chunk(
            [q_ref[0, rows, sl] for sl in lanes], [fz_ref[0, rows, sl] for sl in lanes],
            [v_ref[0, rows, sl] for sl in lanes], [gate_ref[0, rows, sl] for sl in lanes],
            consts, [st_scr[h] for h in range(HGRN_PACK)], masks)
        for h, sl in enumerate(lanes):
            st_scr[h] = st_news[h]
            o_ref[0, rows, sl] = outs[h].astype(BF16)
        return carry

    lax.fori_loop(0, ts // C, body, 0)


def _hgrn_scan(proj, lb, norm_g, ts):
    B, S, _ = proj.shape
    H = HGRN_HEADS
    hd = HGRN_HEAD_DIM
    w = HGRN_PACK * hd
    nblk = H // HGRN_PACK

    def col(section):
        return pl.BlockSpec((1, ts, w), lambda b, h, s: (b, s, h + section * nblk))

    vec = pl.BlockSpec((1, w), lambda b, h, s: (0, h))
    return pl.pallas_call(
        functools.partial(_hgrn_scan_kernel, ts=ts),
        out_shape=jax.ShapeDtypeStruct((B, S, H * hd), BF16),
        grid=(B, nblk, S // ts),
        in_specs=[col(0), col(1), col(2), col(3), vec, vec],
        out_specs=pl.BlockSpec((1, ts, w), lambda b, h, s: (b, s, h)),
        scratch_shapes=[pltpu.VMEM((HGRN_PACK, hd, hd), F32)],
        compiler_params=_cparams("parallel", "parallel", "arbitrary"),
        name="hgrn_scan",
    )(proj, proj, proj, proj, lb, norm_g[None, :])


def _tile(n, want):
    t = min(n, want)
    assert n % t == 0, (n, want)
    return t


def kernel(x, p, positions, norm1_g, norm2_g, mla_w_in, mla_q_norm_g, mla_w_uq, mla_kv_norm_g, mla_w_ukv, mla_w_out, conv_w_pw1, conv_b_pw1, conv_w_dw, conv_b_dw, conv_ln_g, conv_ln_b, conv_w_pw2, conv_b_pw2, hgrn_w_in, hgrn_lb_logits, hgrn_norm_g, hgrn_w_out, ffn_w_gu, ffn_w_down, ple_w_proj, ple_norm_g, ple_w_gate, final_norm_g):
    B, S, D = x.shape
    depth = norm1_g.shape[0]
    T = B * S
    tm = _tile(T, 512)
    ts_proj = _tile(S, 256)
    tq = _tile(S, 512)
    ts_conv = _tile(S, 256)
    ts_scan = _tile(S, 512)
    zero_bias = jnp.zeros((D,), F32)

    lower_bounds = _lower_bounds(hgrn_lb_logits.astype(F32))
    cos, sin = _rope_tables(positions, _tile(S, 512))

    h = x.reshape(T, D)
    for i in range(depth):
        mixer = i % N_MIXERS
        j = i // N_MIXERS
        if mixer == 0:
            q, k, vt = _mla_proj(h.reshape(B, S, D), norm1_g[i], mla_w_in[j], mla_q_norm_g[j], mla_w_uq[j],
                                 mla_kv_norm_g[j], mla_w_ukv[j], cos, sin, ts_proj)
            a = _attention(q, k, vt, tq)
            w_o, b_o = mla_w_out[j], zero_bias
        elif mixer == 1:
            glu = _conv_glu(h, norm1_g[i], conv_w_pw1[j], conv_b_pw1[j], tm, 512)
            a = _conv_dw(glu.reshape(B, S, D), conv_w_dw[j], conv_b_dw[j], conv_ln_g[j], conv_ln_b[j], ts_conv)
            w_o, b_o = conv_w_pw2[j], conv_b_pw2[j]
        else:
            proj = _norm_mm(h, norm1_g[i], hgrn_w_in[j], _tile(T, 1024), 1024)
            a = _hgrn_scan(proj.reshape(B, S, 4 * D), lower_bounds[i:i + 1], hgrn_norm_g[j], ts_scan)
            w_o, b_o = hgrn_w_out[j], zero_bias
        h = _layer_tail(h, a.reshape(T, D), w_o, b_o, norm2_g[i], ffn_w_gu[i], ffn_w_down[i],
                        p[i].reshape(T, PLE_DIM), ple_norm_g[i], ple_w_gate[i], ple_w_proj[i],
                        final_norm_g, i == depth - 1, tm, D_FF // 2)
    return h.reshape(B, S, D)
```

```python
import functools
import math

import jax
import jax.numpy as jnp
from jax import lax
from jax.experimental import pallas as pl
from jax.experimental.pallas import tpu as pltpu

D_MODEL = 1024
N_MIXERS = 3
PLE_DIM = 256
D_FF = 2816
MLA_HEADS = 16
MLA_NOPE = 64
MLA_ROPE = 32
MLA_V = 64
MLA_Q_LORA = 384
MLA_KV_LORA = 256
ROPE_BASE = 10000.0
CONV_WIDTH = 31
HGRN_HEADS = 8
HGRN_HEAD_DIM = 128
CHUNK = 64
EPS = 1e-6

LANES = 128
SUBLANES = 8
HEAD_PAD = 128
LOG2E = 1.4426950408889634
F32 = jnp.float32
BF16 = jnp.bfloat16
NEG_INF = float("-inf")
VMEM_LIMIT = 56 * 1024 * 1024
ROW_CHUNK = 256


def _cparams(*sem):
    return pltpu.CompilerParams(dimension_semantics=sem, vmem_limit_bytes=VMEM_LIMIT)


def _rms(x, g):
    return x * lax.rsqrt(jnp.mean(x * x, axis=-1, keepdims=True) + EPS) * g


def _dot(a, b):
    return jnp.dot(a, b, preferred_element_type=F32)


def _dot_nt(a, b):
    return lax.dot_general(a, b, (((1,), (1,)), ((), ())), preferred_element_type=F32)


def _dot_tn(a, b):
    return lax.dot_general(a, b, (((0,), (0,)), ((), ())), preferred_element_type=F32)


def _sigmoid(x):
    return 1.0 / (1.0 + jnp.exp(-x))


def _rope_table_kernel(pos_ref, invf_ref, sign_ref, cos_ref, sin_ref):
    ang = pos_ref[0].astype(F32) * invf_ref[...]
    cos_ref[0] = jnp.cos(ang)
    sin_ref[0] = jnp.sin(ang) * sign_ref[...]


def _rope_tables(positions, ts):
    B, S = positions.shape
    inv_freq = 1.0 / (ROPE_BASE ** (jnp.arange(0, MLA_ROPE, 2, dtype=F32) / MLA_ROPE))
    half = MLA_ROPE // 2
    zeros_n = jnp.zeros((MLA_NOPE,), F32)
    zeros_p = jnp.zeros((HEAD_PAD - MLA_NOPE - MLA_ROPE,), F32)
    invf = jnp.concatenate([zeros_n, inv_freq, inv_freq, zeros_p])[None, :]
    sign = jnp.concatenate([zeros_n, -jnp.ones((half,), F32), jnp.ones((half,), F32), zeros_p])[None, :]
    vec = pl.BlockSpec((1, HEAD_PAD), lambda b, s: (0, 0))
    tab = pl.BlockSpec((1, ts, HEAD_PAD), lambda b, s: (b, s, 0))
    return pl.pallas_call(
        _rope_table_kernel,
        out_shape=(jax.ShapeDtypeStruct((B, S, HEAD_PAD), F32),) * 2,
        grid=(B, S // ts),
        in_specs=[pl.BlockSpec((1, ts, 1), lambda b, s: (b, s, 0)), vec, vec],
        out_specs=(tab, tab),
        compiler_params=_cparams("parallel", "parallel"),
        name="rope_tables",
    )(positions[:, :, None], invf, sign)


def _rope_apply(x, cos, sin_signed, lane):
    half = MLA_ROPE // 2
    from_hi = pltpu.roll(x, HEAD_PAD - half, axis=1)
    from_lo = pltpu.roll(x, half, axis=1)
    swapped = jnp.where(lane < MLA_NOPE + half, from_hi, from_lo)
    return x * cos + swapped * sin_signed


def _mla_proj_kernel(h_ref, g1_ref, wc_ref, gq_ref, gkv_ref, wuq_ref, wuk_ref, wuv_ref,
                     cos_ref, sin_ref, q_ref, k_ref, v_ref, *, q_scale):
    xn = _rms(h_ref[0], g1_ref[...]).astype(BF16)
    c = _dot(xn, wc_ref[...])
    cq = c[:, :MLA_Q_LORA]
    ckv = c[:, MLA_Q_LORA:MLA_Q_LORA + MLA_KV_LORA]
    kr = c[:, MLA_Q_LORA + MLA_KV_LORA:]
    cqn = _rms(cq, gq_ref[...]).astype(BF16)
    ckvn = _rms(ckv, gkv_ref[...]).astype(BF16)
    cos = cos_ref[0]
    sin = sin_ref[0]
    lane = lax.broadcasted_iota(jnp.int32, cos.shape, 1)
    kr_rot = _rope_apply(kr, cos, sin, lane)
    qf = _dot(cqn, wuq_ref[...])
    kf = _dot(ckvn, wuk_ref[...])
    for hh in range(MLA_HEADS):
        sl = slice(hh * HEAD_PAD, (hh + 1) * HEAD_PAD)
        q_ref[0, hh] = (_rope_apply(qf[:, sl], cos, sin, lane) * q_scale).astype(BF16)
        k_ref[0, hh] = (kf[:, sl] + kr_rot).astype(BF16)
    v_ref[0] = _dot_nt(wuv_ref[...], ckvn).astype(BF16)


def _mla_proj(h, g1, w_in, gq, w_uq, gkv, w_ukv, cos, sin, ts):
    B, S, D = h.shape
    nq, nkv = MLA_Q_LORA, MLA_KV_LORA
    pad = HEAD_PAD - MLA_NOPE - MLA_ROPE
    w_kr = jnp.pad(w_in[:, nq + nkv:], ((0, 0), (MLA_NOPE, pad)))
    wc = jnp.concatenate([w_in[:, :nq + nkv], w_kr], axis=1).astype(BF16)
    wuq = jnp.pad(w_uq.reshape(nq, MLA_HEADS, MLA_NOPE + MLA_ROPE), ((0, 0), (0, 0), (0, pad)))
    wuq = wuq.reshape(nq, MLA_HEADS * HEAD_PAD).astype(BF16)
    wukv = w_ukv.reshape(nkv, MLA_HEADS, MLA_NOPE + MLA_V)
    wuk = jnp.pad(wukv[:, :, :MLA_NOPE], ((0, 0), (0, 0), (0, HEAD_PAD - MLA_NOPE)))
    wuk = wuk.reshape(nkv, MLA_HEADS * HEAD_PAD).astype(BF16)
    wuv = wukv[:, :, MLA_NOPE:].reshape(nkv, MLA_HEADS * MLA_V).T.astype(BF16)
    q_scale = (MLA_NOPE + MLA_ROPE) ** -0.5 * LOG2E

    def full(a):
        return pl.BlockSpec(a.shape, lambda b, s: (0,) * a.ndim)

    g1, gq, gkv = g1[None, :], gq[None, :], gkv[None, :]
    tab = pl.BlockSpec((1, ts, HEAD_PAD), lambda b, s: (b, s, 0))
    head_spec = pl.BlockSpec((1, MLA_HEADS, ts, HEAD_PAD), lambda b, s: (b, 0, s, 0))
    qk_shape = jax.ShapeDtypeStruct((B, MLA_HEADS, S, HEAD_PAD), BF16)
    return pl.pallas_call(
        functools.partial(_mla_proj_kernel, q_scale=q_scale),
        out_shape=(qk_shape, qk_shape, jax.ShapeDtypeStruct((B, MLA_HEADS * MLA_V, S), BF16)),
        grid=(B, S // ts),
        in_specs=[pl.BlockSpec((1, ts, D), lambda b, s: (b, s, 0)), full(g1), full(wc), full(gq),
                  full(gkv), full(wuq), full(wuk), full(wuv), tab, tab],
        out_specs=(head_spec, head_spec,
                   pl.BlockSpec((1, MLA_HEADS * MLA_V, ts), lambda b, s: (b, 0, s))),
        compiler_params=_cparams("parallel", "parallel"),
        name="mla_proj",
    )(h, g1, wc, gq, gkv, wuq, wuk, wuv, cos, sin)


ATTN_ONES_ROWS = 16
ATTN_CHUNKS = 1
ATTN_UNROLL = 4


def _attn_kernel(q_ref, k_ref, vt_ref, o_ref, m_scr, acc_scr, sa_scr, bma_scr, sb_scr, bmb_scr, qt_scr,
                 *, tq):
    qi = pl.program_id(2)
    m_scr[...] = jnp.full(m_scr.shape, NEG_INF, F32)
    acc_scr[...] = jnp.zeros(acc_scr.shape, F32)
    ones_rows = jnp.ones((ATTN_ONES_ROWS, tq), BF16)
    for hh in range(2):
        qt_scr[hh] = q_ref[0, hh].T

    bufs = ((sa_scr, bma_scr), (sb_scr, bmb_scr))

    ck = tq // ATTN_CHUNKS

    def stage(score, value):
        for hh in range(2):
            if value is not None:
                v_blk, v_buf = value
                vs_scr, vbm_scr = bufs[v_buf]
                v_start = pl.multiple_of(v_blk * tq, tq)
                m_prev = m_scr[hh]
                m_new = jnp.maximum(m_prev, vbm_scr[hh])
                alpha = jnp.exp2(m_prev - m_new)
                pv = None
            if score is not None:
                s_blk, s_buf, masked = score
                ss_scr, sbm_scr = bufs[s_buf]
                s_start = pl.multiple_of(s_blk * tq, tq)
                bm = None
            for c in range(ATTN_CHUNKS):
                if score is not None:
                    st = _dot(k_ref[0, hh, pl.ds(s_start + c * ck, ck), :], qt_scr[hh])
                    if masked:
                        key = lax.broadcasted_iota(jnp.int32, st.shape, 0) + c * ck
                        qry = lax.broadcasted_iota(jnp.int32, st.shape, 1)
                        st = jnp.where(key <= qry, st, NEG_INF)
                    ss_scr[hh, c * ck:(c + 1) * ck, :] = st
                    cm = jnp.max(st, axis=0, keepdims=True)
                    bm = cm if bm is None else jnp.maximum(bm, cm)
                if value is not None:
                    pt = jnp.exp2(vs_scr[hh, c * ck:(c + 1) * ck, :] - m_new).astype(BF16)
                    vt = vt_ref[0, hh * MLA_V:(hh + 1) * MLA_V, pl.ds(v_start + c * ck, ck)]
                    vt = jnp.concatenate([vt, ones_rows[:, :ck]], axis=0)
                    d = _dot(vt, pt)
                    pv = d if pv is None else pv + d
            if score is not None:
                sbm_scr[hh] = bm
            if value is not None:
                acc_scr[hh] = alpha * acc_scr[hh] + pv
                m_scr[hh] = m_new

    U = ATTN_UNROLL
    stage((qi, 0, True), None)

    def block_of(n):
        return jnp.where(n == 0, qi, n - 1)

    def run_stages(n0, count, score_last):
        for i in range(count):
            score = (n0 + i, (i + 1) % 2, False) if (i + 1 < count or score_last) else None
            stage(score, (block_of(n0 + i), i % 2))

    trips = qi // U

    def body(t, carry):
        run_stages(U * t, U, True)
        return carry

    lax.fori_loop(0, trips, body, 0)
    for r in range(U):
        @pl.when(qi % U == r)
        def _():
            run_stages(U * trips, r + 1, False)

    ot = jnp.concatenate([acc_scr[hh, :MLA_V] / acc_scr[hh, MLA_V:MLA_V + 1] for hh in range(2)],
                         axis=0)
    o_ref[0] = ot.T.astype(BF16)


def _attention(q, k, vt, tq):
    B, H, S, _ = q.shape
    return pl.pallas_call(
        functools.partial(_attn_kernel, tq=tq),
        out_shape=jax.ShapeDtypeStruct((B, S, H * MLA_V), BF16),
        grid=(B, H // 2, S // tq),
        in_specs=[pl.BlockSpec((1, 2, tq, HEAD_PAD), lambda b, p, i: (b, p, i, 0)),
                  pl.BlockSpec((1, 2, S, HEAD_PAD), lambda b, p, i: (b, p, 0, 0)),
                  pl.BlockSpec((1, 2 * MLA_V, S), lambda b, p, i: (b, p, 0))],
        out_specs=pl.BlockSpec((1, tq, 2 * MLA_V), lambda b, p, i: (b, i, p)),
        scratch_shapes=[pltpu.VMEM((2, 1, tq), F32),
                        pltpu.VMEM((2, MLA_V + ATTN_ONES_ROWS, tq), F32),
                        pltpu.VMEM((2, tq, tq), F32), pltpu.VMEM((2, 1, tq), F32),
                        pltpu.VMEM((2, tq, tq), F32), pltpu.VMEM((2, 1, tq), F32),
                        pltpu.VMEM((2, HEAD_PAD, tq), BF16)],
        compiler_params=_cparams("parallel", "parallel", "arbitrary"),
        name="mla_attention",
    )(q, k, vt)


HGRN_F_SECTION = 1


def _hgrn_proj_kernel(h_ref, g_ref, w_ref, qig_ref, f_ref, xn_scr):
    j = pl.program_id(1)

    @pl.when(j == 0)
    def _():
        xn_scr[...] = _rms(h_ref[...], g_ref[...]).astype(BF16)

    @pl.when(j == HGRN_F_SECTION)
    def _():
        f_ref[...] = _dot(xn_scr[...], w_ref[...])

    @pl.when(j != HGRN_F_SECTION)
    def _():
        qig_ref[...] = _dot(xn_scr[...], w_ref[...]).astype(BF16)


def _hgrn_proj(h, g, w, tm):
    T, D = h.shape
    assert w.shape == (D, 4 * D)

    def qig_block(i, j):
        return (i, jnp.where(j > HGRN_F_SECTION, j - 1, jnp.minimum(j, HGRN_F_SECTION - 1)))

    return pl.pallas_call(
        _hgrn_proj_kernel,
        out_shape=(jax.ShapeDtypeStruct((T, 3 * D), BF16), jax.ShapeDtypeStruct((T, D), F32)),
        grid=(T // tm, 4),
        in_specs=[pl.BlockSpec((tm, D), lambda i, j: (i, 0)), pl.BlockSpec((1, D), lambda i, j: (0, 0)),
                  pl.BlockSpec((D, D), lambda i, j: (0, j))],
        out_specs=(pl.BlockSpec((tm, D), qig_block), pl.BlockSpec((tm, D), lambda i, j: (i, 0))),
        scratch_shapes=[pltpu.VMEM((tm, D), BF16)],
        compiler_params=_cparams("parallel", "arbitrary"),
        name="hgrn_proj",
    )(h, g[None, :], w.astype(BF16))


def _tail_kernel(h_ref, a_ref, wo_ref, bo_ref, g2_ref, wg_ref, wu_ref, wd_ref, p_ref, gp_ref,
                 wgate_ref, wproj_ref, gf_ref, o_ref, xn_scr, acc_scr, *, final_norm):
    f = pl.program_id(1)
    chunks = [slice(r0, r0 + ROW_CHUNK) for r0 in range(0, acc_scr.shape[0], ROW_CHUNK)]

    @pl.when(f == 0)
    def _():
        for rows in chunks:
            h1 = h_ref[rows, :] + _dot(a_ref[rows, :], wo_ref[...]) + bo_ref[...]
            acc_scr[rows, :] = h1
            xn_scr[rows, :] = _rms(h1, g2_ref[...]).astype(BF16)

    for rows in chunks:
        xn = xn_scr[rows, :]
        gate = _dot(xn, wg_ref[...])
        up = _dot(xn, wu_ref[...])
        act = (gate * _sigmoid(gate) * up).astype(BF16)
        acc_scr[rows, :] += _dot(act, wd_ref[...])

    @pl.when(f == pl.num_programs(1) - 1)
    def _():
        for rows in chunks:
            h2 = acc_scr[rows, :]
            pgate = _sigmoid(_dot(_rms(h2, gp_ref[...]).astype(BF16), wgate_ref[...]))
            out = h2 + _dot(p_ref[rows, :].astype(BF16), wproj_ref[...]) * pgate
            if final_norm:
                out = _rms(out, gf_ref[...])
            o_ref[rows, :] = out


def _layer_tail(h, a, w_o, b_o, g2, w_gu, w_down, p, g_ple, w_gate, w_proj, g_final, final_norm, tm, tf):
    T, D = h.shape
    nf = D_FF // tf
    w_gu = w_gu.astype(BF16)

    def tile(w):
        return pl.BlockSpec((tm, w), lambda i, f: (i, 0))

    def resident(shape):
        return pl.BlockSpec(shape, lambda i, f: (0, 0), pipeline_mode=pl.Buffered(1))

    return pl.pallas_call(
        functools.partial(_tail_kernel, final_norm=final_norm),
        out_shape=jax.ShapeDtypeStruct((T, D), F32),
        grid=(T // tm, nf),
        in_specs=[tile(D), tile(D), resident((D, D)), resident((1, D)), resident((1, D)),
                  pl.BlockSpec((D, tf), lambda i, f: (0, f)),
                  pl.BlockSpec((D, tf), lambda i, f: (0, f + nf)),
                  pl.BlockSpec((tf, D), lambda i, f: (f, 0)),
                  tile(PLE_DIM), resident((1, D)), resident((D, D)), resident((PLE_DIM, D)),
                  resident((1, D))],
        out_specs=tile(D),
        scratch_shapes=[pltpu.VMEM((tm, D), BF16), pltpu.VMEM((tm, D), F32)],
        compiler_params=_cparams("parallel", "arbitrary"),
        name="layer_tail",
    )(h, a, w_o.astype(BF16), b_o[None, :], g2[None, :], w_gu, w_gu, w_down.astype(BF16), p,
      g_ple[None, :], w_gate.astype(BF16), w_proj.astype(BF16), g_final[None, :])


def _glu_kernel(h_ref, g_ref, wa_ref, wb_ref, ba_ref, bb_ref, o_ref, xn_scr):
    @pl.when(pl.program_id(1) == 0)
    def _():
        xn_scr[...] = _rms(h_ref[...], g_ref[...]).astype(BF16)

    for r0 in range(0, o_ref.shape[0], ROW_CHUNK):
        xn = xn_scr[r0:r0 + ROW_CHUNK, :]
        a = _dot(xn, wa_ref[...]) + ba_ref[...]
        b = _dot(xn, wb_ref[...]) + bb_ref[...]
        o_ref[r0:r0 + ROW_CHUNK, :] = a * _sigmoid(b)


def _conv_glu(h, g, w_pw1, b_pw1, tm, tn):
    T, D = h.shape
    nn = D // tn
    w = w_pw1.astype(BF16)
    b = b_pw1[None, :]
    return pl.pallas_call(
        _glu_kernel,
        out_shape=jax.ShapeDtypeStruct((T, D), F32),
        grid=(T // tm, nn),
        in_specs=[pl.BlockSpec((tm, D), lambda i, j: (i, 0)), pl.BlockSpec((1, D), lambda i, j: (0, 0)),
                  pl.BlockSpec((D, tn), lambda i, j: (0, j)), pl.BlockSpec((D, tn), lambda i, j: (0, j + nn)),
                  pl.BlockSpec((1, tn), lambda i, j: (0, j)), pl.BlockSpec((1, tn), lambda i, j: (0, j + nn))],
        out_specs=pl.BlockSpec((tm, tn), lambda i, j: (i, j)),
        scratch_shapes=[pltpu.VMEM((tm, D), BF16)],
        compiler_params=_cparams("parallel", "arbitrary"),
        name="conv_glu",
    )(h, g[None, :], w, w, b, b)


CONV_HALO = 32
CONV_ROWS = 64


def _conv_dw_kernel(a_ref, wdw_ref, bdw_ref, lng_ref, lnb_ref, z_ref, ext_scr, y_scr, sh_scr, *, ts):
    @pl.when(pl.program_id(1) == 0)
    def _():
        ext_scr[0:CONV_HALO, :] = jnp.zeros((CONV_HALO, D_MODEL), F32)

    ext_scr[CONV_HALO:, :] = a_ref[0]
    off = CONV_HALO - (CONV_WIDTH - 1)
    rows_sh = sh_scr.shape[1]
    for ph in range(1, SUBLANES):
        sh_scr[ph - 1] = ext_scr[ph:ph + rows_sh, :]
    for r0 in range(0, ts, CONV_ROWS):
        for c0 in range(0, D_MODEL, LANES):
            acc = jnp.zeros((CONV_ROWS, LANES), F32)
            for j in range(CONV_WIDTH):
                grp, ph = divmod(off + j, SUBLANES)
                src = ext_scr if ph == 0 else sh_scr.at[ph - 1]
                start = r0 + grp * SUBLANES
                acc = acc + wdw_ref[j:j + 1, c0:c0 + LANES] * src[start:start + CONV_ROWS, c0:c0 + LANES]
            y_scr[r0:r0 + CONV_ROWS, c0:c0 + LANES] = acc
    ext_scr[0:CONV_HALO, :] = ext_scr[ts:ts + CONV_HALO, :]
    y = y_scr[...] + bdw_ref[...]
    mu = jnp.mean(y, axis=-1, keepdims=True)
    yc = y - mu
    var = jnp.mean(yc * yc, axis=-1, keepdims=True)
    z = yc * lax.rsqrt(var + EPS) * lng_ref[...] + lnb_ref[...]
    z_ref[0] = (z * _sigmoid(z)).astype(BF16)


def _conv_dw(a, w_dw, b_dw, ln_g, ln_b, ts):
    B, S, D = a.shape

    def full(x):
        return pl.BlockSpec(x.shape, lambda b, s: (0,) * x.ndim)

    tile = pl.BlockSpec((1, ts, D), lambda b, s: (b, s, 0))
    args = (w_dw, b_dw[None, :], ln_g[None, :], ln_b[None, :])
    return pl.pallas_call(
        functools.partial(_conv_dw_kernel, ts=ts),
        out_shape=jax.ShapeDtypeStruct((B, S, D), BF16),
        grid=(B, S // ts),
        in_specs=[tile] + [full(x) for x in args],
        out_specs=tile,
        scratch_shapes=[pltpu.VMEM((ts + CONV_HALO, D), F32), pltpu.VMEM((ts, D), F32),
                        pltpu.VMEM((SUBLANES - 1, ts + CONV_HALO - SUBLANES, D), F32)],
        compiler_params=_cparams("parallel", "arbitrary"),
        name="conv_depthwise",
    )(a, *args)


def _lower_bounds_kernel(x_ref, o_ref):
    x = x_ref[...]
    n = x.shape[0]
    m = x[0:1]
    for r in range(1, n):
        m = jnp.maximum(m, x[r:r + 1])
    e = jnp.exp(x - m)
    tot = e[0:1]
    for r in range(1, n):
        tot = tot + e[r:r + 1]
    sm = e / tot
    cum = sm[0:1]
    first = cum
    o_ref[0:1, :] = cum - first
    for r in range(1, n):
        cum = cum + sm[r:r + 1]
        o_ref[r:r + 1, :] = cum - first


def _lower_bounds(logits):
    return pl.pallas_call(
        _lower_bounds_kernel,
        out_shape=jax.ShapeDtypeStruct(logits.shape, F32),
        name="hgrn_lower_bounds",
    )(logits)


SUB = SUBLANES
NGRP = CHUNK // SUB
HGRN_LEVELS = (32, 16, 8)
HGRN_PACK = 2


def _groups(x):
    return [x[g * SUB:(g + 1) * SUB, :] for g in range(NGRP)]


def _cumsum_chunk(x, row):
    gs = _groups(x)
    for step in (1, 2, 4):
        gs = [g + jnp.where(row >= step, pltpu.roll(g, step, axis=0), 0.0) for g in gs]
    out = [gs[0]]
    run = gs[0][SUB - 1:SUB, :]
    for g in range(1, NGRP):
        out.append(gs[g] + run)
        run = run + gs[g][SUB - 1:SUB, :]
    return jnp.concatenate(out, axis=0)


def _level_operands(m, bg, qg, kg):
    zeros = jnp.zeros((SUB, LANES), F32)
    qs, ks = [], []
    for g in range(NGRP):
        blk = (g * SUB) // m
        if blk % 2 == 1:
            ref = bg[blk * m // SUB - 1][SUB - 1:SUB, :]
            qs.append(qg[g] * jnp.exp2(bg[g] - ref))
            ks.append(zeros)
        else:
            ref = bg[(blk + 1) * m // SUB - 1][SUB - 1:SUB, :]
            qs.append(zeros)
            ks.append(kg[g] * jnp.exp2(ref - bg[g]))
    return jnp.concatenate(qs, axis=0).astype(BF16), jnp.concatenate(ks, axis=0).astype(BF16)


def _hgrn_chunk(qs, fzs, vs, gates, consts, sts, masks):
    C = CHUNK
    row, pair_masks, diag_masks = masks
    heads = range(len(qs))
    lfs = []
    for h in heads:
        t1, l1p, _ = consts[h]
        log_sig = jnp.minimum(fzs[h], 0.0) - jnp.log(1.0 + jnp.exp(-jnp.abs(fzs[h])))
        t2 = l1p + log_sig
        lfs.append(jnp.maximum(t1, t2) + jnp.log(1.0 + jnp.exp(-jnp.abs(t1 - t2))))
    lfs = [lf * LOG2E for lf in lfs]
    kks = [1.0 - jnp.exp2(lfs[h]) for h in heads]
    bs = [_cumsum_chunk(lfs[h], row) for h in heads]
    b_lasts = [bs[h][C - 1:C, :] for h in heads]
    vbs = [vs[h].astype(BF16) for h in heads]

    os_ = [_dot_nt((qs[h] * jnp.exp2(bs[h])).astype(BF16), sts[h].astype(BF16)) for h in heads]
    kds = [(kks[h] * jnp.exp2(b_lasts[h] - bs[h])).astype(BF16) for h in heads]
    st_news = [sts[h] * jnp.exp2(b_lasts[h]) + _dot_tn(vbs[h], kds[h]) for h in heads]

    bgs = [_groups(bs[h]) for h in heads]
    qgs = [_groups(qs[h]) for h in heads]
    kgs = [_groups(kks[h]) for h in heads]
    attns = [jnp.zeros((C, C), F32) for _ in heads]
    for m, pair in zip(HGRN_LEVELS, pair_masks):
        ops = [_level_operands(m, bgs[h], qgs[h], kgs[h]) for h in heads]
        for h in heads:
            attns[h] = attns[h] + jnp.where(pair, _dot_nt(*ops[h]), 0.0)

    for d in range(SUB):
        for h in heads:
            if d == 0:
                w = qs[h] * kks[h]
            else:
                valid = row >= d
                w = jnp.concatenate(
                    [qgs[h][g] * pltpu.roll(kgs[h][g], d, axis=0)
                     * jnp.exp2(jnp.where(valid, bgs[h][g] - pltpu.roll(bgs[h][g], d, axis=0), NEG_INF))
                     for g in range(NGRP)], axis=0)
            val = jnp.sum(w, axis=1, keepdims=True)
            attns[h] = attns[h] + jnp.where(diag_masks[d], val, 0.0)

    outs = []
    for h in heads:
        o = os_[h] + _dot(attns[h].astype(BF16), vbs[h])
        outs.append(_rms(o, consts[h][2]) * _sigmoid(gates[h]))
    return outs, st_news


def _hgrn_scan_kernel(q_ref, fz_ref, v_ref, gate_ref, lb_ref, ng_ref, o_ref, st_scr, *, ts):
    @pl.when(pl.program_id(2) == 0)
    def _():
        st_scr[...] = jnp.zeros(st_scr.shape, F32)

    C = CHUNK
    hd = HGRN_HEAD_DIM
    lanes = [slice(h * hd, (h + 1) * hd) for h in range(HGRN_PACK)]
    consts = []
    for sl in lanes:
        lb = lb_ref[:, sl]
        consts.append((jnp.log(lb), jnp.log1p(-lb), ng_ref[:, sl]))
    row = lax.broadcasted_iota(jnp.int32, (SUB, LANES), 0)
    trow = lax.broadcasted_iota(jnp.int32, (C, C), 0)
    scol = lax.broadcasted_iota(jnp.int32, (C, C), 1)
    pair_masks = []
    for m in HGRN_LEVELS:
        sh = m.bit_length() - 1
        pair_masks.append((((trow >> sh) & 1) == 1) & ((scol >> sh) == (trow >> sh) - 1))
    diag_masks = [trow - scol == d for d in range(SUB)]
    masks = (row, pair_masks, diag_masks)

    def body(c, carry):
        r0 = pl.multiple_of(c * C, C)
        rows = pl.ds(r0, C)
        outs, st_news = _hgrn_chunk(
            [q_ref[0, rows, sl].astype(F32) for sl in lanes], [fz_ref[0, rows, sl] for sl in lanes],
            [v_ref[0, rows, sl] for sl in lanes], [gate_ref[0, rows, sl].astype(F32) for sl in lanes],
            consts, [st_scr[h] for h in range(HGRN_PACK)], masks)
        for h, sl in enumerate(lanes):
            st_scr[h] = st_news[h]
            o_ref[0, rows, sl] = outs[h].astype(BF16)
        return carry

    lax.fori_loop(0, ts // C, body, 0)


def _hgrn_scan(qig, fz, lb, norm_g, ts):
    B, S, _ = fz.shape
    H = HGRN_HEADS
    hd = HGRN_HEAD_DIM
    w = HGRN_PACK * hd
    nblk = H // HGRN_PACK

    def col(section):
        return pl.BlockSpec((1, ts, w), lambda b, h, s: (b, s, h + section * nblk))

    vec = pl.BlockSpec((1, w), lambda b, h, s: (0, h))
    return pl.pallas_call(
        functools.partial(_hgrn_scan_kernel, ts=ts),
        out_shape=jax.ShapeDtypeStruct((B, S, H * hd), BF16),
        grid=(B, nblk, S // ts),
        in_specs=[col(0), col(0), col(1), col(2), vec, vec],
        out_specs=pl.BlockSpec((1, ts, w), lambda b, h, s: (b, s, h)),
        scratch_shapes=[pltpu.VMEM((HGRN_PACK, hd, hd), F32)],
        compiler_params=_cparams("parallel", "parallel", "arbitrary"),
        name="hgrn_scan",
    )(qig, fz, qig, qig, lb, norm_g[None, :])


def _tile(n, want):
    t = min(n, want)
    assert n % t == 0, (n, want)
    return t


def kernel(x, p, positions, norm1_g, norm2_g, mla_w_in, mla_q_norm_g, mla_w_uq, mla_kv_norm_g, mla_w_ukv, mla_w_out, conv_w_pw1, conv_b_pw1, conv_w_dw, conv_b_dw, conv_ln_g, conv_ln_b, conv_w_pw2, conv_b_pw2, hgrn_w_in, hgrn_lb_logits, hgrn_norm_g, hgrn_w_out, ffn_w_gu, ffn_w_down, ple_w_proj, ple_norm_g, ple_w_gate, final_norm_g):
    B, S, D = x.shape
    depth = norm1_g.shape[0]
    T = B * S
    tm = _tile(T, 512)
    ts_proj = _tile(S, 512)
    tq = _tile(S, 512)
    ts_conv = _tile(S, 256)
    ts_scan = _tile(S, 512)
    zero_bias = jnp.zeros((D,), F32)

    lower_bounds = _lower_bounds(hgrn_lb_logits.astype(F32))
    cos, sin = _rope_tables(positions, _tile(S, 512))

    h = x.reshape(T, D)
    for i in range(depth):
        mixer = i % N_MIXERS
        j = i // N_MIXERS
        if mixer == 0:
            q, k, vt = _mla_proj(h.reshape(B, S, D), norm1_g[i], mla_w_in[j], mla_q_norm_g[j], mla_w_uq[j],
                                 mla_kv_norm_g[j], mla_w_ukv[j], cos, sin, ts_proj)
            a = _attention(q, k, vt, tq)
            w_o, b_o = mla_w_out[j], zero_bias
        elif mixer == 1:
            glu = _conv_glu(h, norm1_g[i], conv_w_pw1[j], conv_b_pw1[j], _tile(T, 1024), 512)
            a = _conv_dw(glu.reshape(B, S, D), conv_w_dw[j], conv_b_dw[j], conv_ln_g[j], conv_ln_b[j], ts_conv)
            w_o, b_o = conv_w_pw2[j], conv_b_pw2[j]
        else:
            qig, fz = _hgrn_proj(h, norm1_g[i], hgrn_w_in[j], _tile(T, 1024))
            a = _hgrn_scan(qig.reshape(B, S, 3 * D), fz.reshape(B, S, D), lower_bounds[i:i + 1],
                           hgrn_norm_g[j], ts_scan)
            w_o, b_o = hgrn_w_out[j], zero_bias
        h = _layer_tail(h, a.reshape(T, D), w_o, b_o, norm2_g[i], ffn_w_gu[i], ffn_w_down[i],
                        p[i].reshape(T, PLE_DIM), ple_norm_g[i], ple_w_gate[i], ple_w_proj[i],
                        final_norm_g, i == depth - 1, tm, D_FF // 2)
    return h.reshape(B, S, D)
```

```python
import functools
import math

import jax
import jax.numpy as jnp
from jax import lax
from jax.experimental import pallas as pl
from jax.experimental.pallas import tpu as pltpu

D_MODEL = 1024
N_MIXERS = 3
PLE_DIM = 256
D_FF = 2816
MLA_HEADS = 16
MLA_NOPE = 64
MLA_ROPE = 32
MLA_V = 64
MLA_Q_LORA = 384
MLA_KV_LORA = 256
ROPE_BASE = 10000.0
CONV_WIDTH = 31
HGRN_HEADS = 8
HGRN_HEAD_DIM = 128
CHUNK = 64
EPS = 1e-6

LANES = 128
SUBLANES = 8
HEAD_PAD = 128
LOG2E = 1.4426950408889634
F32 = jnp.float32
BF16 = jnp.bfloat16
NEG_INF = float("-inf")
VMEM_LIMIT = 56 * 1024 * 1024
ROW_CHUNK = 256


def _cparams(*sem):
    return pltpu.CompilerParams(dimension_semantics=sem, vmem_limit_bytes=VMEM_LIMIT)


def _rms(x, g):
    return x * lax.rsqrt(jnp.mean(x * x, axis=-1, keepdims=True) + EPS) * g


def _dot(a, b):
    return jnp.dot(a, b, preferred_element_type=F32)


def _dot_nt(a, b):
    return lax.dot_general(a, b, (((1,), (1,)), ((), ())), preferred_element_type=F32)


def _dot_tn(a, b):
    return lax.dot_general(a, b, (((0,), (0,)), ((), ())), preferred_element_type=F32)


def _sigmoid(x):
    return 1.0 / (1.0 + jnp.exp(-x))


def _rope_table_kernel(pos_ref, invf_ref, sign_ref, cos_ref, sin_ref):
    ang = pos_ref[0].astype(F32) * invf_ref[...]
    cos_ref[0] = jnp.cos(ang)
    sin_ref[0] = jnp.sin(ang) * sign_ref[...]


def _rope_tables(positions, ts):
    B, S = positions.shape
    inv_freq = 1.0 / (ROPE_BASE ** (jnp.arange(0, MLA_ROPE, 2, dtype=F32) / MLA_ROPE))
    half = MLA_ROPE // 2
    zeros_n = jnp.zeros((MLA_NOPE,), F32)
    zeros_p = jnp.zeros((HEAD_PAD - MLA_NOPE - MLA_ROPE,), F32)
    invf = jnp.concatenate([zeros_n, inv_freq, inv_freq, zeros_p])[None, :]
    sign = jnp.concatenate([zeros_n, -jnp.ones((half,), F32), jnp.ones((half,), F32), zeros_p])[None, :]
    vec = pl.BlockSpec((1, HEAD_PAD), lambda b, s: (0, 0))
    tab = pl.BlockSpec((1, ts, HEAD_PAD), lambda b, s: (b, s, 0))
    return pl.pallas_call(
        _rope_table_kernel,
        out_shape=(jax.ShapeDtypeStruct((B, S, HEAD_PAD), F32),) * 2,
        grid=(B, S // ts),
        in_specs=[pl.BlockSpec((1, ts, 1), lambda b, s: (b, s, 0)), vec, vec],
        out_specs=(tab, tab),
        compiler_params=_cparams("parallel", "parallel"),
        name="rope_tables",
    )(positions[:, :, None], invf, sign)


def _rope_apply(x, cos, sin_signed, lane):
    half = MLA_ROPE // 2
    from_hi = pltpu.roll(x, HEAD_PAD - half, axis=1)
    from_lo = pltpu.roll(x, half, axis=1)
    swapped = jnp.where(lane < MLA_NOPE + half, from_hi, from_lo)
    return x * cos + swapped * sin_signed


def _mla_proj_kernel(h_ref, g1_ref, wc_ref, gq_ref, gkv_ref, wuq_ref, wuk_ref, wuv_ref,
                     cos_ref, sin_ref, q_ref, k_ref, v_ref, *, q_scale):
    xn = _rms(h_ref[0], g1_ref[...]).astype(BF16)
    c = _dot(xn, wc_ref[...])
    cq = c[:, :MLA_Q_LORA]
    ckv = c[:, MLA_Q_LORA:MLA_Q_LORA + MLA_KV_LORA]
    kr = c[:, MLA_Q_LORA + MLA_KV_LORA:]
    cqn = _rms(cq, gq_ref[...]).astype(BF16)
    ckvn = _rms(ckv, gkv_ref[...]).astype(BF16)
    cos = cos_ref[0]
    sin = sin_ref[0]
    lane = lax.broadcasted_iota(jnp.int32, cos.shape, 1)
    kr_rot = _rope_apply(kr, cos, sin, lane)
    qf = _dot(cqn, wuq_ref[...])
    kf = _dot(ckvn, wuk_ref[...])
    for hh in range(MLA_HEADS):
        sl = slice(hh * HEAD_PAD, (hh + 1) * HEAD_PAD)
        q_ref[0, hh] = (_rope_apply(qf[:, sl], cos, sin, lane) * q_scale).astype(BF16)
        k_ref[0, hh] = (kf[:, sl] + kr_rot).astype(BF16)
    v_ref[0] = _dot_nt(wuv_ref[...], ckvn).astype(BF16)


def _mla_proj(h, g1, w_in, gq, w_uq, gkv, w_ukv, cos, sin, ts):
    B, S, D = h.shape
    nq, nkv = MLA_Q_LORA, MLA_KV_LORA
    pad = HEAD_PAD - MLA_NOPE - MLA_ROPE
    w_kr = jnp.pad(w_in[:, nq + nkv:], ((0, 0), (MLA_NOPE, pad)))
    wc = jnp.concatenate([w_in[:, :nq + nkv], w_kr], axis=1).astype(BF16)
    wuq = jnp.pad(w_uq.reshape(nq, MLA_HEADS, MLA_NOPE + MLA_ROPE), ((0, 0), (0, 0), (0, pad)))
    wuq = wuq.reshape(nq, MLA_HEADS * HEAD_PAD).astype(BF16)
    wukv = w_ukv.reshape(nkv, MLA_HEADS, MLA_NOPE + MLA_V)
    wuk = jnp.pad(wukv[:, :, :MLA_NOPE], ((0, 0), (0, 0), (0, HEAD_PAD - MLA_NOPE)))
    wuk = wuk.reshape(nkv, MLA_HEADS * HEAD_PAD).astype(BF16)
    wuv = wukv[:, :, MLA_NOPE:].reshape(nkv, MLA_HEADS * MLA_V).T.astype(BF16)
    q_scale = (MLA_NOPE + MLA_ROPE) ** -0.5 * LOG2E

    def full(a):
        return pl.BlockSpec(a.shape, lambda b, s: (0,) * a.ndim)

    g1, gq, gkv = g1[None, :], gq[None, :], gkv[None, :]
    tab = pl.BlockSpec((1, ts, HEAD_PAD), lambda b, s: (b, s, 0))
    head_spec = pl.BlockSpec((1, MLA_HEADS, ts, HEAD_PAD), lambda b, s: (b, 0, s, 0))
    qk_shape = jax.ShapeDtypeStruct((B, MLA_HEADS, S, HEAD_PAD), BF16)
    return pl.pallas_call(
        functools.partial(_mla_proj_kernel, q_scale=q_scale),
        out_shape=(qk_shape, qk_shape, jax.ShapeDtypeStruct((B, MLA_HEADS * MLA_V, S), BF16)),
        grid=(B, S // ts),
        in_specs=[pl.BlockSpec((1, ts, D), lambda b, s: (b, s, 0)), full(g1), full(wc), full(gq),
                  full(gkv), full(wuq), full(wuk), full(wuv), tab, tab],
        out_specs=(head_spec, head_spec,
                   pl.BlockSpec((1, MLA_HEADS * MLA_V, ts), lambda b, s: (b, 0, s))),
        compiler_params=_cparams("parallel", "parallel"),
        name="mla_proj",
    )(h, g1, wc, gq, gkv, wuq, wuk, wuv, cos, sin)


ATTN_ONES_ROWS = 16
ATTN_CHUNKS = 1
ATTN_UNROLL = 4


def _attn_kernel(q_ref, k_ref, vt_ref, o_ref, m_scr, acc_scr, sa_scr, bma_scr, sb_scr, bmb_scr, qt_scr,
                 *, tq):
    qi = pl.program_id(2)
    m_scr[...] = jnp.full(m_scr.shape, NEG_INF, F32)
    acc_scr[...] = jnp.zeros(acc_scr.shape, F32)
    ones_rows = jnp.ones((ATTN_ONES_ROWS, tq), BF16)
    for hh in range(2):
        qt_scr[hh] = q_ref[0, hh].T

    bufs = ((sa_scr, bma_scr), (sb_scr, bmb_scr))

    ck = tq // ATTN_CHUNKS

    def stage(score, value):
        for hh in range(2):
            if value is not None:
                v_blk, v_buf = value
                vs_scr, vbm_scr = bufs[v_buf]
                v_start = pl.multiple_of(v_blk * tq, tq)
                m_prev = m_scr[hh]
                m_new = jnp.maximum(m_prev, vbm_scr[hh])
                alpha = jnp.exp2(m_prev - m_new)
                pv = None
            if score is not None:
                s_blk, s_buf, masked = score
                ss_scr, sbm_scr = bufs[s_buf]
                s_start = pl.multiple_of(s_blk * tq, tq)
                bm = None
            for c in range(ATTN_CHUNKS):
                if score is not None:
                    st = _dot(k_ref[0, hh, pl.ds(s_start + c * ck, ck), :], qt_scr[hh])
                    if masked:
                        key = lax.broadcasted_iota(jnp.int32, st.shape, 0) + c * ck
                        qry = lax.broadcasted_iota(jnp.int32, st.shape, 1)
                        st = jnp.where(key <= qry, st, NEG_INF)
                    ss_scr[hh, c * ck:(c + 1) * ck, :] = st
                    cm = jnp.max(st, axis=0, keepdims=True)
                    bm = cm if bm is None else jnp.maximum(bm, cm)
                if value is not None:
                    pt = jnp.exp2(vs_scr[hh, c * ck:(c + 1) * ck, :] - m_new).astype(BF16)
                    vt = vt_ref[0, hh * MLA_V:(hh + 1) * MLA_V, pl.ds(v_start + c * ck, ck)]
                    vt = jnp.concatenate([vt, ones_rows[:, :ck]], axis=0)
                    d = _dot(vt, pt)
                    pv = d if pv is None else pv + d
            if score is not None:
                sbm_scr[hh] = bm
            if value is not None:
                acc_scr[hh] = alpha * acc_scr[hh] + pv
                m_scr[hh] = m_new

    U = ATTN_UNROLL
    stage((qi, 0, True), None)

    def block_of(n):
        return jnp.where(n == 0, qi, n - 1)

    def run_stages(n0, count, score_last):
        for i in range(count):
            score = (n0 + i, (i + 1) % 2, False) if (i + 1 < count or score_last) else None
            stage(score, (block_of(n0 + i), i % 2))

    trips = qi // U

    def body(t, carry):
        run_stages(U * t, U, True)
        return carry

    lax.fori_loop(0, trips, body, 0)
    for r in range(U):
        @pl.when(qi % U == r)
        def _():
            run_stages(U * trips, r + 1, False)

    ot = jnp.concatenate([acc_scr[hh, :MLA_V] / acc_scr[hh, MLA_V:MLA_V + 1] for hh in range(2)],
                         axis=0)
    o_ref[0] = ot.T.astype(BF16)


def _attention(q, k, vt, tq):
    B, H, S, _ = q.shape
    return pl.pallas_call(
        functools.partial(_attn_kernel, tq=tq),
        out_shape=jax.ShapeDtypeStruct((B, S, H * MLA_V), BF16),
        grid=(B, H // 2, S // tq),
        in_specs=[pl.BlockSpec((1, 2, tq, HEAD_PAD), lambda b, p, i: (b, p, i, 0)),
                  pl.BlockSpec((1, 2, S, HEAD_PAD), lambda b, p, i: (b, p, 0, 0)),
                  pl.BlockSpec((1, 2 * MLA_V, S), lambda b, p, i: (b, p, 0))],
        out_specs=pl.BlockSpec((1, tq, 2 * MLA_V), lambda b, p, i: (b, i, p)),
        scratch_shapes=[pltpu.VMEM((2, 1, tq), F32),
                        pltpu.VMEM((2, MLA_V + ATTN_ONES_ROWS, tq), F32),
                        pltpu.VMEM((2, tq, tq), F32), pltpu.VMEM((2, 1, tq), F32),
                        pltpu.VMEM((2, tq, tq), F32), pltpu.VMEM((2, 1, tq), F32),
                        pltpu.VMEM((2, HEAD_PAD, tq), BF16)],
        compiler_params=_cparams("parallel", "parallel", "arbitrary"),
        name="mla_attention",
    )(q, k, vt)


HGRN_F_SECTION = 1


def _hgrn_proj_kernel(h_ref, g_ref, w_ref, lb_ref, qig_ref, lf_ref, xn_scr):
    j = pl.program_id(1)

    @pl.when(j == 0)
    def _():
        xn_scr[...] = _rms(h_ref[...], g_ref[...]).astype(BF16)

    @pl.when(j == HGRN_F_SECTION)
    def _():
        lb = lb_ref[...]
        t1 = jnp.log(lb)
        l1p = jnp.log1p(-lb)
        for r0 in range(0, lf_ref.shape[0], ROW_CHUNK):
            fz = _dot(xn_scr[r0:r0 + ROW_CHUNK, :], w_ref[...])
            log_sig = jnp.minimum(fz, 0.0) - jnp.log(1.0 + jnp.exp(-jnp.abs(fz)))
            t2 = l1p + log_sig
            lf = jnp.maximum(t1, t2) + jnp.log(1.0 + jnp.exp(-jnp.abs(t1 - t2)))
            lf_ref[r0:r0 + ROW_CHUNK, :] = lf * LOG2E

    @pl.when(j != HGRN_F_SECTION)
    def _():
        qig_ref[...] = _dot(xn_scr[...], w_ref[...]).astype(BF16)


def _hgrn_proj(h, g, w, lb, tm):
    T, D = h.shape
    assert w.shape == (D, 4 * D)

    def qig_block(i, j):
        return (i, jnp.where(j > HGRN_F_SECTION, j - 1, jnp.minimum(j, HGRN_F_SECTION - 1)))

    return pl.pallas_call(
        _hgrn_proj_kernel,
        out_shape=(jax.ShapeDtypeStruct((T, 3 * D), BF16), jax.ShapeDtypeStruct((T, D), F32)),
        grid=(T // tm, 4),
        in_specs=[pl.BlockSpec((tm, D), lambda i, j: (i, 0)), pl.BlockSpec((1, D), lambda i, j: (0, 0)),
                  pl.BlockSpec((D, D), lambda i, j: (0, j)), pl.BlockSpec((1, D), lambda i, j: (0, 0))],
        out_specs=(pl.BlockSpec((tm, D), qig_block), pl.BlockSpec((tm, D), lambda i, j: (i, 0))),
        scratch_shapes=[pltpu.VMEM((tm, D), BF16)],
        compiler_params=_cparams("parallel", "arbitrary"),
        name="hgrn_proj",
    )(h, g[None, :], w.astype(BF16), lb)


def _tail_kernel(h_ref, a_ref, wo_ref, bo_ref, g2_ref, wg_ref, wu_ref, wd_ref, p_ref, gp_ref,
                 wgate_ref, wproj_ref, gf_ref, o_ref, xn_scr, acc_scr, *, final_norm):
    f = pl.program_id(1)
    chunks = [slice(r0, r0 + ROW_CHUNK) for r0 in range(0, acc_scr.shape[0], ROW_CHUNK)]

    @pl.when(f == 0)
    def _():
        for rows in chunks:
            h1 = h_ref[rows, :] + _dot(a_ref[rows, :], wo_ref[...]) + bo_ref[...]
            acc_scr[rows, :] = h1
            xn_scr[rows, :] = _rms(h1, g2_ref[...]).astype(BF16)

    for rows in chunks:
        xn = xn_scr[rows, :]
        gate = _dot(xn, wg_ref[...])
        up = _dot(xn, wu_ref[...])
        act = (gate * _sigmoid(gate) * up).astype(BF16)
        acc_scr[rows, :] += _dot(act, wd_ref[...])

    @pl.when(f == pl.num_programs(1) - 1)
    def _():
        for rows in chunks:
            h2 = acc_scr[rows, :]
            pgate = _sigmoid(_dot(_rms(h2, gp_ref[...]).astype(BF16), wgate_ref[...]))
            out = h2 + _dot(p_ref[rows, :].astype(BF16), wproj_ref[...]) * pgate
            if final_norm:
                out = _rms(out, gf_ref[...])
            o_ref[rows, :] = out


def _layer_tail(h, a, w_o, b_o, g2, w_gu, w_down, p_all, layer, g_ple, w_gate, w_proj, g_final,
                final_norm, tm, tf):
    T, D = h.shape
    nf = D_FF // tf
    w_gu = w_gu.astype(BF16)
    p_row0 = layer * (T // tm)

    def tile(w):
        return pl.BlockSpec((tm, w), lambda i, f: (i, 0))

    def resident(shape):
        return pl.BlockSpec(shape, lambda i, f: (0, 0), pipeline_mode=pl.Buffered(1))

    return pl.pallas_call(
        functools.partial(_tail_kernel, final_norm=final_norm),
        out_shape=jax.ShapeDtypeStruct((T, D), F32),
        grid=(T // tm, nf),
        in_specs=[tile(D), tile(D), resident((D, D)), resident((1, D)), resident((1, D)),
                  pl.BlockSpec((D, tf), lambda i, f: (0, f)),
                  pl.BlockSpec((D, tf), lambda i, f: (0, f + nf)),
                  pl.BlockSpec((tf, D), lambda i, f: (f, 0)),
                  pl.BlockSpec((tm, PLE_DIM), lambda i, f: (p_row0 + i, 0)),
                  resident((1, D)), resident((D, D)), resident((PLE_DIM, D)), resident((1, D))],
        out_specs=tile(D),
        scratch_shapes=[pltpu.VMEM((tm, D), BF16), pltpu.VMEM((tm, D), F32)],
        compiler_params=_cparams("parallel", "arbitrary"),
        name="layer_tail",
    )(h, a, w_o.astype(BF16), b_o[None, :], g2[None, :], w_gu, w_gu, w_down.astype(BF16), p_all,
      g_ple[None, :], w_gate.astype(BF16), w_proj.astype(BF16), g_final[None, :])


def _glu_kernel(h_ref, g_ref, wa_ref, wb_ref, ba_ref, bb_ref, o_ref, xn_scr):
    @pl.when(pl.program_id(1) == 0)
    def _():
        xn_scr[...] = _rms(h_ref[...], g_ref[...]).astype(BF16)

    for r0 in range(0, o_ref.shape[0], ROW_CHUNK):
        xn = xn_scr[r0:r0 + ROW_CHUNK, :]
        a = _dot(xn, wa_ref[...]) + ba_ref[...]
        b = _dot(xn, wb_ref[...]) + bb_ref[...]
        o_ref[r0:r0 + ROW_CHUNK, :] = a * _sigmoid(b)


def _conv_glu(h, g, w_pw1, b_pw1, tm, tn):
    T, D = h.shape
    nn = D // tn
    w = w_pw1.astype(BF16)
    b = b_pw1[None, :]
    return pl.pallas_call(
        _glu_kernel,
        out_shape=jax.ShapeDtypeStruct((T, D), F32),
        grid=(T // tm, nn),
        in_specs=[pl.BlockSpec((tm, D), lambda i, j: (i, 0)), pl.BlockSpec((1, D), lambda i, j: (0, 0)),
                  pl.BlockSpec((D, tn), lambda i, j: (0, j)), pl.BlockSpec((D, tn), lambda i, j: (0, j + nn)),
                  pl.BlockSpec((1, tn), lambda i, j: (0, j)), pl.BlockSpec((1, tn), lambda i, j: (0, j + nn))],
        out_specs=pl.BlockSpec((tm, tn), lambda i, j: (i, j)),
        scratch_shapes=[pltpu.VMEM((tm, D), BF16)],
        compiler_params=_cparams("parallel", "arbitrary"),
        name="conv_glu",
    )(h, g[None, :], w, w, b, b)


CONV_HALO = 32
CONV_ROWS = 32


def _conv_dw_kernel(a_ref, wdw_ref, bdw_ref, lng_ref, lnb_ref, z_ref, ext_scr, y_scr, sh_scr, *, ts):
    @pl.when(pl.program_id(1) == 0)
    def _():
        ext_scr[0:CONV_HALO, :] = jnp.zeros((CONV_HALO, D_MODEL), F32)

    ext_scr[CONV_HALO:, :] = a_ref[0]
    off = CONV_HALO - (CONV_WIDTH - 1)
    rows_sh = sh_scr.shape[1]
    for ph in range(1, SUBLANES):
        sh_scr[ph - 1] = ext_scr[ph:ph + rows_sh, :]
    for c0 in range(0, D_MODEL, LANES):
        cols = slice(c0, c0 + LANES)
        wtap = [jnp.broadcast_to(wdw_ref[j:j + 1, cols], (SUBLANES, LANES)) for j in range(CONV_WIDTH)]
        for r0 in range(0, ts, CONV_ROWS):
            groups = range(CONV_ROWS // SUBLANES)
            acc = [jnp.zeros((SUBLANES, LANES), F32) for _ in groups]
            for j in range(CONV_WIDTH):
                grp, ph = divmod(off + j, SUBLANES)
                src = ext_scr if ph == 0 else sh_scr.at[ph - 1]
                for g in groups:
                    start = r0 + (grp + g) * SUBLANES
                    acc[g] = acc[g] + wtap[j] * src[start:start + SUBLANES, cols]
            for g in groups:
                y_scr[r0 + g * SUBLANES:r0 + (g + 1) * SUBLANES, cols] = acc[g]
    ext_scr[0:CONV_HALO, :] = ext_scr[ts:ts + CONV_HALO, :]
    y = y_scr[...] + bdw_ref[...]
    mu = jnp.mean(y, axis=-1, keepdims=True)
    yc = y - mu
    var = jnp.mean(yc * yc, axis=-1, keepdims=True)
    z = yc * lax.rsqrt(var + EPS) * lng_ref[...] + lnb_ref[...]
    z_ref[0] = (z * _sigmoid(z)).astype(BF16)


def _conv_dw(a, w_dw, b_dw, ln_g, ln_b, ts):
    B, S, D = a.shape

    def full(x):
        return pl.BlockSpec(x.shape, lambda b, s: (0,) * x.ndim)

    tile = pl.BlockSpec((1, ts, D), lambda b, s: (b, s, 0))
    args = (w_dw, b_dw[None, :], ln_g[None, :], ln_b[None, :])
    return pl.pallas_call(
        functools.partial(_conv_dw_kernel, ts=ts),
        out_shape=jax.ShapeDtypeStruct((B, S, D), BF16),
        grid=(B, S // ts),
        in_specs=[tile] + [full(x) for x in args],
        out_specs=tile,
        scratch_shapes=[pltpu.VMEM((ts + CONV_HALO, D), F32), pltpu.VMEM((ts, D), F32),
                        pltpu.VMEM((SUBLANES - 1, ts + CONV_HALO - SUBLANES, D), F32)],
        compiler_params=_cparams("parallel", "arbitrary"),
        name="conv_depthwise",
    )(a, *args)


def _lower_bounds_kernel(x_ref, o_ref):
    x = x_ref[...]
    n = x.shape[0]
    m = x[0:1]
    for r in range(1, n):
        m = jnp.maximum(m, x[r:r + 1])
    e = jnp.exp(x - m)
    tot = e[0:1]
    for r in range(1, n):
        tot = tot + e[r:r + 1]
    sm = e / tot
    cum = sm[0:1]
    first = cum
    o_ref[0:1, :] = cum - first
    for r in range(1, n):
        cum = cum + sm[r:r + 1]
        o_ref[r:r + 1, :] = cum - first


def _lower_bounds(logits):
    return pl.pallas_call(
        _lower_bounds_kernel,
        out_shape=jax.ShapeDtypeStruct(logits.shape, F32),
        name="hgrn_lower_bounds",
    )(logits)


SUB = SUBLANES
NGRP = CHUNK // SUB
HGRN_LEVELS = (32, 16, 8)
HGRN_PACK = 4


def _groups(x):
    return [x[g * SUB:(g + 1) * SUB, :] for g in range(NGRP)]


def _cumsum_chunk(x, row):
    gs = _groups(x)
    for step in (1, 2, 4):
        gs = [g + jnp.where(row >= step, pltpu.roll(g, step, axis=0), 0.0) for g in gs]
    out = [gs[0]]
    run = gs[0][SUB - 1:SUB, :]
    for g in range(1, NGRP):
        out.append(gs[g] + run)
        run = run + gs[g][SUB - 1:SUB, :]
    return jnp.concatenate(out, axis=0)


def _level_operands(m, bg, qg, kg):
    zeros = jnp.zeros((SUB, LANES), F32)
    qs, ks = [], []
    for g in range(NGRP):
        blk = (g * SUB) // m
        if blk % 2 == 1:
            ref = bg[blk * m // SUB - 1][SUB - 1:SUB, :]
            qs.append(qg[g] * jnp.exp2(bg[g] - ref))
            ks.append(zeros)
        else:
            ref = bg[(blk + 1) * m // SUB - 1][SUB - 1:SUB, :]
            qs.append(zeros)
            ks.append(kg[g] * jnp.exp2(ref - bg[g]))
    return jnp.concatenate(qs, axis=0).astype(BF16), jnp.concatenate(ks, axis=0).astype(BF16)


def _hgrn_chunk(qs, lfs, vs, gates, gains, sts, masks):
    C = CHUNK
    row, pair_masks, diag_masks = masks
    heads = range(len(qs))
    kks = [1.0 - jnp.exp2(lfs[h]) for h in heads]
    bs = [_cumsum_chunk(lfs[h], row) for h in heads]
    b_lasts = [bs[h][C - 1:C, :] for h in heads]
    vbs = [vs[h].astype(BF16) for h in heads]

    os_ = [_dot_nt((qs[h] * jnp.exp2(bs[h])).astype(BF16), sts[h].astype(BF16)) for h in heads]
    kds = [(kks[h] * jnp.exp2(b_lasts[h] - bs[h])).astype(BF16) for h in heads]
    st_news = [sts[h] * jnp.exp2(b_lasts[h]) + _dot_tn(vbs[h], kds[h]) for h in heads]

    bgs = [_groups(bs[h]) for h in heads]
    qgs = [_groups(qs[h]) for h in heads]
    kgs = [_groups(kks[h]) for h in heads]
    attns = [jnp.zeros((C, C), F32) for _ in heads]
    for m, pair in zip(HGRN_LEVELS, pair_masks):
        ops = [_level_operands(m, bgs[h], qgs[h], kgs[h]) for h in heads]
        for h in heads:
            attns[h] = jnp.where(pair, _dot_nt(*ops[h]), attns[h])

    for d in range(SUB):
        for h in heads:
            if d == 0:
                w = qs[h] * kks[h]
            else:
                w = jnp.concatenate(
                    [qgs[h][g] * pltpu.roll(kgs[h][g], d, axis=0)
                     * jnp.exp2(bgs[h][g] - pltpu.roll(bgs[h][g], d, axis=0))
                     for g in range(NGRP)], axis=0)
            val = jnp.sum(w, axis=1, keepdims=True)
            attns[h] = jnp.where(diag_masks[d], val, attns[h])

    outs = []
    for h in heads:
        o = os_[h] + _dot(attns[h].astype(BF16), vbs[h])
        outs.append(_rms(o, gains[h]) * _sigmoid(gates[h]))
    return outs, st_news


def _hgrn_scan_kernel(q_ref, lf_ref, v_ref, gate_ref, ng_ref, o_ref, st_scr, *, ts):
    @pl.when(pl.program_id(2) == 0)
    def _():
        st_scr[...] = jnp.zeros(st_scr.shape, F32)

    C = CHUNK
    hd = HGRN_HEAD_DIM
    lanes = [slice(h * hd, (h + 1) * hd) for h in range(HGRN_PACK)]
    gains = [ng_ref[:, sl] for sl in lanes]
    row = lax.broadcasted_iota(jnp.int32, (SUB, LANES), 0)
    trow = lax.broadcasted_iota(jnp.int32, (C, C), 0)
    scol = lax.broadcasted_iota(jnp.int32, (C, C), 1)
    pair_masks = []
    for m in HGRN_LEVELS:
        sh = m.bit_length() - 1
        pair_masks.append((((trow >> sh) & 1) == 1) & ((scol >> sh) == (trow >> sh) - 1))
    diag_masks = [(trow - scol == d) & ((trow & (SUB - 1)) >= d) for d in range(SUB)]
    masks = (row, pair_masks, diag_masks)

    def body(c, carry):
        r0 = pl.multiple_of(c * C, C)
        rows = pl.ds(r0, C)
        outs, st_news = _hgrn_chunk(
            [q_ref[0, rows, sl].astype(F32) for sl in lanes], [lf_ref[0, rows, sl] for sl in lanes],
            [v_ref[0, rows, sl] for sl in lanes], [gate_ref[0, rows, sl].astype(F32) for sl in lanes],
            gains, [st_scr[h] for h in range(HGRN_PACK)], masks)
        for h, sl in enumerate(lanes):
            st_scr[h] = st_news[h]
            o_ref[0, rows, sl] = outs[h].astype(BF16)
        return carry

    lax.fori_loop(0, ts // C, body, 0)


def _hgrn_scan(qig, lf, norm_g, ts):
    B, S, _ = lf.shape
    H = HGRN_HEADS
    hd = HGRN_HEAD_DIM
    w = HGRN_PACK * hd
    nblk = H // HGRN_PACK

    def col(section):
        return pl.BlockSpec((1, ts, w), lambda b, h, s: (b, s, h + section * nblk))

    vec = pl.BlockSpec((1, w), lambda b, h, s: (0, h))
    return pl.pallas_call(
        functools.partial(_hgrn_scan_kernel, ts=ts),
        out_shape=jax.ShapeDtypeStruct((B, S, H * hd), BF16),
        grid=(B, nblk, S // ts),
        in_specs=[col(0), col(0), col(1), col(2), vec],
        out_specs=pl.BlockSpec((1, ts, w), lambda b, h, s: (b, s, h)),
        scratch_shapes=[pltpu.VMEM((HGRN_PACK, hd, hd), F32)],
        compiler_params=_cparams("parallel", "parallel", "arbitrary"),
        name="hgrn_scan",
    )(qig, lf, qig, qig, norm_g[None, :])


def _tile(n, want):
    t = min(n, want)
    assert n % t == 0, (n, want)
    return t


def kernel(x, p, positions, norm1_g, norm2_g, mla_w_in, mla_q_norm_g, mla_w_uq, mla_kv_norm_g, mla_w_ukv, mla_w_out, conv_w_pw1, conv_b_pw1, conv_w_dw, conv_b_dw, conv_ln_g, conv_ln_b, conv_w_pw2, conv_b_pw2, hgrn_w_in, hgrn_lb_logits, hgrn_norm_g, hgrn_w_out, ffn_w_gu, ffn_w_down, ple_w_proj, ple_norm_g, ple_w_gate, final_norm_g):
    B, S, D = x.shape
    depth = norm1_g.shape[0]
    T = B * S
    tm = _tile(T, 512)
    ts_proj = _tile(S, 512)
    tq = _tile(S, 512)
    ts_conv = _tile(S, 256)
    ts_scan = _tile(S, 512)
    zero_bias = jnp.zeros((D,), F32)

    lower_bounds = _lower_bounds(hgrn_lb_logits.astype(F32))
    cos, sin = _rope_tables(positions, _tile(S, 512))

    h = x.reshape(T, D)
    p_all = p.reshape(depth * T, PLE_DIM)
    for i in range(depth):
        mixer = i % N_MIXERS
        j = i // N_MIXERS
        if mixer == 0:
            q, k, vt = _mla_proj(h.reshape(B, S, D), norm1_g[i], mla_w_in[j], mla_q_norm_g[j], mla_w_uq[j],
                                 mla_kv_norm_g[j], mla_w_ukv[j], cos, sin, ts_proj)
            a = _attention(q, k, vt, tq)
            w_o, b_o = mla_w_out[j], zero_bias
        elif mixer == 1:
            glu = _conv_glu(h, norm1_g[i], conv_w_pw1[j], conv_b_pw1[j], _tile(T, 1024), 512)
            a = _conv_dw(glu.reshape(B, S, D), conv_w_dw[j], conv_b_dw[j], conv_ln_g[j], conv_ln_b[j], ts_conv)
            w_o, b_o = conv_w_pw2[j], conv_b_pw2[j]
        else:
            qig, lf = _hgrn_proj(h, norm1_g[i], hgrn_w_in[j], lower_bounds[i:i + 1], _tile(T, 1024))
            a = _hgrn_scan(qig.reshape(B, S, 3 * D), lf.reshape(B, S, D), hgrn_norm_g[j], ts_scan)
            w_o, b_o = hgrn_w_out[j], zero_bias
        h = _layer_tail(h, a.reshape(T, D), w_o, b_o, norm2_g[i], ffn_w_gu[i], ffn_w_down[i],
                        p_all, i, ple_norm_g[i], ple_w_gate[i], ple_w_proj[i],
                        final_norm_g, i == depth - 1, tm, D_FF // 2)
    return h.reshape(B, S, D)
```

```python
import functools
import math

import jax
import jax.numpy as jnp
from jax import lax
from jax.experimental import pallas as pl
from jax.experimental.pallas import tpu as pltpu

D_MODEL = 1024
N_MIXERS = 3
PLE_DIM = 256
D_FF = 2816
MLA_HEADS = 16
MLA_NOPE = 64
MLA_ROPE = 32
MLA_V = 64
MLA_Q_LORA = 384
MLA_KV_LORA = 256
ROPE_BASE = 10000.0
CONV_WIDTH = 31
HGRN_HEADS = 8
HGRN_HEAD_DIM = 128
CHUNK = 64
EPS = 1e-6

LANES = 128
SUBLANES = 8
HEAD_PAD = 128
LOG2E = 1.4426950408889634
F32 = jnp.float32
BF16 = jnp.bfloat16
NEG_INF = float("-inf")
VMEM_LIMIT = 56 * 1024 * 1024
ROW_CHUNK = 256


def _cparams(*sem):
    return pltpu.CompilerParams(dimension_semantics=sem, vmem_limit_bytes=VMEM_LIMIT)


def _rms(x, g):
    return x * lax.rsqrt(jnp.mean(x * x, axis=-1, keepdims=True) + EPS) * g


def _dot(a, b):
    return jnp.dot(a, b, preferred_element_type=F32)


def _dot_nt(a, b):
    return lax.dot_general(a, b, (((1,), (1,)), ((), ())), preferred_element_type=F32)


def _dot_tn(a, b):
    return lax.dot_general(a, b, (((0,), (0,)), ((), ())), preferred_element_type=F32)


def _sigmoid(x):
    return 1.0 / (1.0 + jnp.exp(-x))


def _rope_table_kernel(pos_ref, invf_ref, sign_ref, cos_ref, sin_ref):
    ang = pos_ref[0].astype(F32) * invf_ref[...]
    cos_ref[0] = jnp.cos(ang)
    sin_ref[0] = jnp.sin(ang) * sign_ref[...]


def _head_lanes(nope, rope):
    ref = nope if nope is not None else rope
    half_r, half_n, gap = MLA_ROPE // 2, MLA_NOPE // 2, (HEAD_PAD - MLA_NOPE - MLA_ROPE) // 2

    def z(n):
        return jnp.zeros(ref.shape[:-1] + (n,), ref.dtype)

    n0, n1 = (nope[..., :half_n], nope[..., half_n:]) if nope is not None else (z(half_n), z(half_n))
    x1, x2 = (rope[..., :half_r], rope[..., half_r:]) if rope is not None else (z(half_r), z(half_r))
    out = jnp.concatenate([n0, x1, n1, z(gap), x2, z(gap)], axis=-1)
    assert out.shape[-1] == HEAD_PAD and half_n + half_r + half_n + gap == HEAD_PAD // 2 + half_n
    return out


def _rope_tables(positions, ts):
    B, S = positions.shape
    inv_freq = 1.0 / (ROPE_BASE ** (jnp.arange(0, MLA_ROPE, 2, dtype=F32) / MLA_ROPE))
    half = MLA_ROPE // 2
    invf = _head_lanes(None, jnp.concatenate([inv_freq, inv_freq]))[None, :]
    sign = _head_lanes(None, jnp.concatenate([-jnp.ones((half,), F32), jnp.ones((half,), F32)]))[None, :]
    vec = pl.BlockSpec((1, HEAD_PAD), lambda b, s: (0, 0))
    tab = pl.BlockSpec((1, ts, HEAD_PAD), lambda b, s: (b, s, 0))
    return pl.pallas_call(
        _rope_table_kernel,
        out_shape=(jax.ShapeDtypeStruct((B, S, HEAD_PAD), F32),) * 2,
        grid=(B, S // ts),
        in_specs=[pl.BlockSpec((1, ts, 1), lambda b, s: (b, s, 0)), vec, vec],
        out_specs=(tab, tab),
        compiler_params=_cparams("parallel", "parallel"),
        name="rope_tables",
    )(positions[:, :, None], invf, sign)


def _rope_apply(x, cos, sin_signed):
    return x * cos + pltpu.roll(x, HEAD_PAD // 2, axis=1) * sin_signed


def _mla_proj_kernel(h_ref, g1_ref, wc_ref, gq_ref, gkv_ref, wuq_ref, wuk_ref, wuv_ref,
                     cos_ref, sin_ref, q_ref, k_ref, v_ref, *, q_scale):
    for r0 in range(0, h_ref.shape[1], ROW_CHUNK):
        rows = slice(r0, r0 + ROW_CHUNK)
        xn = _rms(h_ref[0, rows, :], g1_ref[...]).astype(BF16)
        c = _dot(xn, wc_ref[...])
        cq = c[:, :MLA_Q_LORA]
        ckv = c[:, MLA_Q_LORA:MLA_Q_LORA + MLA_KV_LORA]
        kr = c[:, MLA_Q_LORA + MLA_KV_LORA:]
        cqn = _rms(cq, gq_ref[...]).astype(BF16)
        ckvn = _rms(ckv, gkv_ref[...]).astype(BF16)
        cos = cos_ref[0, rows, :]
        sin = sin_ref[0, rows, :]
        kr_rot = _rope_apply(kr, cos, sin)
        qf = _dot(cqn, wuq_ref[...])
        kf = _dot(ckvn, wuk_ref[...])
        for hh in range(MLA_HEADS):
            sl = slice(hh * HEAD_PAD, (hh + 1) * HEAD_PAD)
            q_ref[0, hh, rows, :] = (_rope_apply(qf[:, sl], cos, sin) * q_scale).astype(BF16)
            k_ref[0, hh, rows, :] = (kf[:, sl] + kr_rot).astype(BF16)
        v_ref[0, :, rows] = _dot_nt(wuv_ref[...], ckvn).astype(BF16)


def _mla_proj(h, g1, w_in, gq, w_uq, gkv, w_ukv, cos, sin, ts):
    B, S, D = h.shape
    nq, nkv = MLA_Q_LORA, MLA_KV_LORA
    w_kr = _head_lanes(None, w_in[:, nq + nkv:])
    wc = jnp.concatenate([w_in[:, :nq + nkv], w_kr], axis=1).astype(BF16)
    wuq = w_uq.reshape(nq, MLA_HEADS, MLA_NOPE + MLA_ROPE)
    wuq = _head_lanes(wuq[:, :, :MLA_NOPE], wuq[:, :, MLA_NOPE:])
    wuq = wuq.reshape(nq, MLA_HEADS * HEAD_PAD).astype(BF16)
    wukv = w_ukv.reshape(nkv, MLA_HEADS, MLA_NOPE + MLA_V)
    wuk = _head_lanes(wukv[:, :, :MLA_NOPE], None)
    wuk = wuk.reshape(nkv, MLA_HEADS * HEAD_PAD).astype(BF16)
    wuv = wukv[:, :, MLA_NOPE:].reshape(nkv, MLA_HEADS * MLA_V).T.astype(BF16)
    q_scale = (MLA_NOPE + MLA_ROPE) ** -0.5 * LOG2E

    def full(a):
        return pl.BlockSpec(a.shape, lambda b, s: (0,) * a.ndim)

    g1, gq, gkv = g1[None, :], gq[None, :], gkv[None, :]
    tab = pl.BlockSpec((1, ts, HEAD_PAD), lambda b, s: (b, s, 0))
    head_spec = pl.BlockSpec((1, MLA_HEADS, ts, HEAD_PAD), lambda b, s: (b, 0, s, 0))
    qk_shape = jax.ShapeDtypeStruct((B, MLA_HEADS, S, HEAD_PAD), BF16)
    return pl.pallas_call(
        functools.partial(_mla_proj_kernel, q_scale=q_scale),
        out_shape=(qk_shape, qk_shape, jax.ShapeDtypeStruct((B, MLA_HEADS * MLA_V, S), BF16)),
        grid=(B, S // ts),
        in_specs=[pl.BlockSpec((1, ts, D), lambda b, s: (b, s, 0)), full(g1), full(wc), full(gq),
                  full(gkv), full(wuq), full(wuk), full(wuv), tab, tab],
        out_specs=(head_spec, head_spec,
                   pl.BlockSpec((1, MLA_HEADS * MLA_V, ts), lambda b, s: (b, 0, s))),
        compiler_params=_cparams("parallel", "parallel"),
        name="mla_proj",
    )(h, g1, wc, gq, gkv, wuq, wuk, wuv, cos, sin)


ATTN_ONES_ROWS = 16
ATTN_CHUNKS = 1
ATTN_UNROLL = 4


def _attn_kernel(q_ref, k_ref, vt_ref, o_ref, m_scr, acc_scr, sa_scr, bma_scr, sb_scr, bmb_scr, qt_scr,
                 *, tq):
    qi = pl.program_id(2)
    m_scr[...] = jnp.full(m_scr.shape, NEG_INF, F32)
    acc_scr[...] = jnp.zeros(acc_scr.shape, F32)
    ones_rows = jnp.ones((ATTN_ONES_ROWS, tq), BF16)
    for hh in range(2):
        qt_scr[hh] = q_ref[0, hh].T

    bufs = ((sa_scr, bma_scr), (sb_scr, bmb_scr))

    ck = tq // ATTN_CHUNKS

    def stage(score, value):
        for hh in range(2):
            if value is not None:
                v_blk, v_buf = value
                vs_scr, vbm_scr = bufs[v_buf]
                v_start = pl.multiple_of(v_blk * tq, tq)
                m_prev = m_scr[hh]
                m_new = jnp.maximum(m_prev, vbm_scr[hh])
                alpha = jnp.exp2(m_prev - m_new)
                pv = None
            if score is not None:
                s_blk, s_buf, masked = score
                ss_scr, sbm_scr = bufs[s_buf]
                s_start = pl.multiple_of(s_blk * tq, tq)
                bm = None
            for c in range(ATTN_CHUNKS):
                if score is not None:
                    st = _dot(k_ref[0, hh, pl.ds(s_start + c * ck, ck), :], qt_scr[hh])
                    if masked:
                        key = lax.broadcasted_iota(jnp.int32, st.shape, 0) + c * ck
                        qry = lax.broadcasted_iota(jnp.int32, st.shape, 1)
                        st = jnp.where(key <= qry, st, NEG_INF)
                    ss_scr[hh, c * ck:(c + 1) * ck, :] = st
                    cm = jnp.max(st, axis=0, keepdims=True)
                    bm = cm if bm is None else jnp.maximum(bm, cm)
                if value is not None:
                    pt = jnp.exp2(vs_scr[hh, c * ck:(c + 1) * ck, :] - m_new).astype(BF16)
                    vt = vt_ref[0, hh * MLA_V:(hh + 1) * MLA_V, pl.ds(v_start + c * ck, ck)]
                    vt = jnp.concatenate([vt, ones_rows[:, :ck]], axis=0)
                    d = _dot(vt, pt)
                    pv = d if pv is None else pv + d
            if score is not None:
                sbm_scr[hh] = bm
            if value is not None:
                acc_scr[hh] = alpha * acc_scr[hh] + pv
                m_scr[hh] = m_new

    U = ATTN_UNROLL
    stage((qi, 0, True), None)

    def block_of(n):
        return jnp.where(n == 0, qi, n - 1)

    def run_stages(n0, count, score_last):
        for i in range(count):
            score = (n0 + i, (i + 1) % 2, False) if (i + 1 < count or score_last) else None
            stage(score, (block_of(n0 + i), i % 2))

    trips = qi // U

    def body(t, carry):
        run_stages(U * t, U, True)
        return carry

    lax.fori_loop(0, trips, body, 0)
    for r in range(U):
        @pl.when(qi % U == r)
        def _():
            run_stages(U * trips, r + 1, False)

    ot = jnp.concatenate([acc_scr[hh, :MLA_V] / acc_scr[hh, MLA_V:MLA_V + 1] for hh in range(2)],
                         axis=0)
    o_ref[0] = ot.T.astype(BF16)


def _attention(q, k, vt, tq):
    B, H, S, _ = q.shape
    return pl.pallas_call(
        functools.partial(_attn_kernel, tq=tq),
        out_shape=jax.ShapeDtypeStruct((B, S, H * MLA_V), BF16),
        grid=(B, H // 2, S // tq),
        in_specs=[pl.BlockSpec((1, 2, tq, HEAD_PAD), lambda b, p, i: (b, p, i, 0)),
                  pl.BlockSpec((1, 2, S, HEAD_PAD), lambda b, p, i: (b, p, 0, 0)),
                  pl.BlockSpec((1, 2 * MLA_V, S), lambda b, p, i: (b, p, 0))],
        out_specs=pl.BlockSpec((1, tq, 2 * MLA_V), lambda b, p, i: (b, i, p)),
        scratch_shapes=[pltpu.VMEM((2, 1, tq), F32),
                        pltpu.VMEM((2, MLA_V + ATTN_ONES_ROWS, tq), F32),
                        pltpu.VMEM((2, tq, tq), F32), pltpu.VMEM((2, 1, tq), F32),
                        pltpu.VMEM((2, tq, tq), F32), pltpu.VMEM((2, 1, tq), F32),
                        pltpu.VMEM((2, HEAD_PAD, tq), BF16)],
        compiler_params=_cparams("parallel", "parallel", "arbitrary"),
        name="mla_attention",
    )(q, k, vt)


HGRN_F_SECTION = 1


def _hgrn_proj_kernel(h_ref, g_ref, w_ref, lb_ref, qig_ref, lf_ref, xn_scr):
    j = pl.program_id(1)

    @pl.when(j == 0)
    def _():
        xn_scr[...] = _rms(h_ref[...], g_ref[...]).astype(BF16)

    @pl.when(j == HGRN_F_SECTION)
    def _():
        lb = lb_ref[...]
        t1 = jnp.log(lb)
        l1p = jnp.log1p(-lb)
        for r0 in range(0, lf_ref.shape[0], ROW_CHUNK):
            fz = _dot(xn_scr[r0:r0 + ROW_CHUNK, :], w_ref[...])
            log_sig = jnp.minimum(fz, 0.0) - jnp.log(1.0 + jnp.exp(-jnp.abs(fz)))
            t2 = l1p + log_sig
            lf = jnp.maximum(t1, t2) + jnp.log(1.0 + jnp.exp(-jnp.abs(t1 - t2)))
            lf_ref[r0:r0 + ROW_CHUNK, :] = lf * LOG2E

    @pl.when(j != HGRN_F_SECTION)
    def _():
        qig_ref[...] = _dot(xn_scr[...], w_ref[...]).astype(BF16)


def _hgrn_proj(h, g, w, lb, tm):
    T, D = h.shape
    assert w.shape == (D, 4 * D)

    def qig_block(i, j):
        return (i, jnp.where(j > HGRN_F_SECTION, j - 1, jnp.minimum(j, HGRN_F_SECTION - 1)))

    return pl.pallas_call(
        _hgrn_proj_kernel,
        out_shape=(jax.ShapeDtypeStruct((T, 3 * D), BF16), jax.ShapeDtypeStruct((T, D), F32)),
        grid=(T // tm, 4),
        in_specs=[pl.BlockSpec((tm, D), lambda i, j: (i, 0)), pl.BlockSpec((1, D), lambda i, j: (0, 0)),
                  pl.BlockSpec((D, D), lambda i, j: (0, j)), pl.BlockSpec((1, D), lambda i, j: (0, 0))],
        out_specs=(pl.BlockSpec((tm, D), qig_block), pl.BlockSpec((tm, D), lambda i, j: (i, 0))),
        scratch_shapes=[pltpu.VMEM((tm, D), BF16)],
        compiler_params=_cparams("parallel", "arbitrary"),
        name="hgrn_proj",
    )(h, g[None, :], w.astype(BF16), lb)


def _tail_kernel(h_ref, a_ref, wo_ref, bo_ref, g2_ref, wgu_ref, wd_ref, p_ref, gp_ref,
                 wgate_ref, wproj_ref, gf_ref, o_ref, *, final_norm):
    for r0 in range(0, o_ref.shape[0], ROW_CHUNK):
        rows = slice(r0, r0 + ROW_CHUNK)
        h1 = h_ref[rows, :] + _dot(a_ref[rows, :], wo_ref[...]) + bo_ref[...]
        xn = _rms(h1, g2_ref[...]).astype(BF16)
        gate = _dot(xn, wgu_ref[:, :D_FF])
        up = _dot(xn, wgu_ref[:, D_FF:])
        act = (gate * _sigmoid(gate) * up).astype(BF16)
        h2 = h1 + _dot(act, wd_ref[...])
        pgate = _sigmoid(_dot(_rms(h2, gp_ref[...]).astype(BF16), wgate_ref[...]))
        out = h2 + _dot(p_ref[rows, :].astype(BF16), wproj_ref[...]) * pgate
        if final_norm:
            out = _rms(out, gf_ref[...])
        o_ref[rows, :] = out


def _layer_tail(h, a, w_o, b_o, g2, w_gu, w_down, p_all, layer, g_ple, w_gate, w_proj, g_final,
                final_norm, tm):
    T, D = h.shape
    p_row0 = layer * (T // tm)

    def tile(w):
        return pl.BlockSpec((tm, w), lambda i: (i, 0))

    def resident(shape):
        return pl.BlockSpec(shape, lambda i: (0, 0), pipeline_mode=pl.Buffered(1))

    return pl.pallas_call(
        functools.partial(_tail_kernel, final_norm=final_norm),
        out_shape=jax.ShapeDtypeStruct((T, D), F32),
        grid=(T // tm,),
        in_specs=[tile(D), tile(D), resident((D, D)), resident((1, D)), resident((1, D)),
                  resident((D, 2 * D_FF)), resident((D_FF, D)),
                  pl.BlockSpec((tm, PLE_DIM), lambda i: (p_row0 + i, 0)),
                  resident((1, D)), resident((D, D)), resident((PLE_DIM, D)), resident((1, D))],
        out_specs=tile(D),
        compiler_params=_cparams("parallel"),
        name="layer_tail",
    )(h, a, w_o.astype(BF16), b_o[None, :], g2[None, :], w_gu.astype(BF16), w_down.astype(BF16), p_all,
      g_ple[None, :], w_gate.astype(BF16), w_proj.astype(BF16), g_final[None, :])


def _glu_kernel(h_ref, g_ref, wa_ref, wb_ref, ba_ref, bb_ref, o_ref, xn_scr):
    @pl.when(pl.program_id(1) == 0)
    def _():
        xn_scr[...] = _rms(h_ref[...], g_ref[...]).astype(BF16)

    for r0 in range(0, o_ref.shape[0], ROW_CHUNK):
        xn = xn_scr[r0:r0 + ROW_CHUNK, :]
        a = _dot(xn, wa_ref[...]) + ba_ref[...]
        b = _dot(xn, wb_ref[...]) + bb_ref[...]
        o_ref[r0:r0 + ROW_CHUNK, :] = a * _sigmoid(b)


def _conv_glu(h, g, w_pw1, b_pw1, tm, tn):
    T, D = h.shape
    nn = D // tn
    w = w_pw1.astype(BF16)
    b = b_pw1[None, :]
    return pl.pallas_call(
        _glu_kernel,
        out_shape=jax.ShapeDtypeStruct((T, D), F32),
        grid=(T // tm, nn),
        in_specs=[pl.BlockSpec((tm, D), lambda i, j: (i, 0)), pl.BlockSpec((1, D), lambda i, j: (0, 0)),
                  pl.BlockSpec((D, tn), lambda i, j: (0, j)), pl.BlockSpec((D, tn), lambda i, j: (0, j + nn)),
                  pl.BlockSpec((1, tn), lambda i, j: (0, j)), pl.BlockSpec((1, tn), lambda i, j: (0, j + nn))],
        out_specs=pl.BlockSpec((tm, tn), lambda i, j: (i, j)),
        scratch_shapes=[pltpu.VMEM((tm, D), BF16)],
        compiler_params=_cparams("parallel", "arbitrary"),
        name="conv_glu",
    )(h, g[None, :], w, w, b, b)


CONV_HALO = 32
CONV_ROWS = 32


def _conv_dw_kernel(a_ref, wdw_ref, bdw_ref, lng_ref, lnb_ref, z_ref, ext_scr, y_scr, sh_scr, *, ts):
    @pl.when(pl.program_id(1) == 0)
    def _():
        ext_scr[0:CONV_HALO, :] = jnp.zeros((CONV_HALO, D_MODEL), F32)

    ext_scr[CONV_HALO:, :] = a_ref[0]
    off = CONV_HALO - (CONV_WIDTH - 1)
    rows_sh = sh_scr.shape[1]
    for ph in range(1, SUBLANES):
        sh_scr[ph - 1] = ext_scr[ph:ph + rows_sh, :]
    for c0 in range(0, D_MODEL, LANES):
        cols = slice(c0, c0 + LANES)
        wtap = [jnp.broadcast_to(wdw_ref[j:j + 1, cols], (SUBLANES, LANES)) for j in range(CONV_WIDTH)]
        for r0 in range(0, ts, CONV_ROWS):
            groups = range(CONV_ROWS // SUBLANES)
            acc = [jnp.zeros((SUBLANES, LANES), F32) for _ in groups]
            for j in range(CONV_WIDTH):
                grp, ph = divmod(off + j, SUBLANES)
                src = ext_scr if ph == 0 else sh_scr.at[ph - 1]
                for g in groups:
                    start = r0 + (grp + g) * SUBLANES
                    acc[g] = acc[g] + wtap[j] * src[start:start + SUBLANES, cols]
            for g in groups:
                y_scr[r0 + g * SUBLANES:r0 + (g + 1) * SUBLANES, cols] = acc[g]
    ext_scr[0:CONV_HALO, :] = ext_scr[ts:ts + CONV_HALO, :]
    y = y_scr[...] + bdw_ref[...]
    mu = jnp.mean(y, axis=-1, keepdims=True)
    yc = y - mu
    var = jnp.mean(yc * yc, axis=-1, keepdims=True)
    z = yc * lax.rsqrt(var + EPS) * lng_ref[...] + lnb_ref[...]
    z_ref[0] = (z * _sigmoid(z)).astype(BF16)


def _conv_dw(a, w_dw, b_dw, ln_g, ln_b, ts):
    B, S, D = a.shape

    def full(x):
        return pl.BlockSpec(x.shape, lambda b, s: (0,) * x.ndim)

    tile = pl.BlockSpec((1, ts, D), lambda b, s: (b, s, 0))
    args = (w_dw, b_dw[None, :], ln_g[None, :], ln_b[None, :])
    return pl.pallas_call(
        functools.partial(_conv_dw_kernel, ts=ts),
        out_shape=jax.ShapeDtypeStruct((B, S, D), BF16),
        grid=(B, S // ts),
        in_specs=[tile] + [full(x) for x in args],
        out_specs=tile,
        scratch_shapes=[pltpu.VMEM((ts + CONV_HALO, D), F32), pltpu.VMEM((ts, D), F32),
                        pltpu.VMEM((SUBLANES - 1, ts + CONV_HALO - SUBLANES, D), F32)],
        compiler_params=_cparams("parallel", "arbitrary"),
        name="conv_depthwise",
    )(a, *args)


def _lower_bounds_kernel(x_ref, o_ref):
    x = x_ref[...]
    n = x.shape[0]
    m = x[0:1]
    for r in range(1, n):
        m = jnp.maximum(m, x[r:r + 1])
    e = jnp.exp(x - m)
    tot = e[0:1]
    for r in range(1, n):
        tot = tot + e[r:r + 1]
    sm = e / tot
    cum = sm[0:1]
    first = cum
    o_ref[0:1, :] = cum - first
    for r in range(1, n):
        cum = cum + sm[r:r + 1]
        o_ref[r:r + 1, :] = cum - first


def _lower_bounds(logits):
    return pl.pallas_call(
        _lower_bounds_kernel,
        out_shape=jax.ShapeDtypeStruct(logits.shape, F32),
        name="hgrn_lower_bounds",
    )(logits)


SUB = SUBLANES
NGRP = CHUNK // SUB
HGRN_LEVELS = (32, 16, 8)
HGRN_PACK = 4


def _groups(x):
    return [x[g * SUB:(g + 1) * SUB, :] for g in range(NGRP)]


def _cumsum_chunk(x, row):
    gs = _groups(x)
    for step in (1, 2, 4):
        gs = [g + jnp.where(row >= step, pltpu.roll(g, step, axis=0), 0.0) for g in gs]
    out = [gs[0]]
    run = gs[0][SUB - 1:SUB, :]
    for g in range(1, NGRP):
        out.append(gs[g] + run)
        run = run + gs[g][SUB - 1:SUB, :]
    return jnp.concatenate(out, axis=0)


def _level_operands(m, bg, qg, kg):
    zeros = jnp.zeros((SUB, LANES), F32)
    qs, ks = [], []
    for g in range(NGRP):
        blk = (g * SUB) // m
        if blk % 2 == 1:
            ref = bg[blk * m // SUB - 1][SUB - 1:SUB, :]
            qs.append(qg[g] * jnp.exp2(bg[g] - ref))
            ks.append(zeros)
        else:
            ref = bg[(blk + 1) * m // SUB - 1][SUB - 1:SUB, :]
            qs.append(zeros)
            ks.append(kg[g] * jnp.exp2(ref - bg[g]))
    return jnp.concatenate(qs, axis=0).astype(BF16), jnp.concatenate(ks, axis=0).astype(BF16)


def _hgrn_chunk(qs, lfs, vs, gates, gains, sts, masks):
    C = CHUNK
    row, pair_masks, diag_masks = masks
    heads = range(len(qs))
    kks = [1.0 - jnp.exp2(lfs[h]) for h in heads]
    bs = [_cumsum_chunk(lfs[h], row) for h in heads]
    b_lasts = [bs[h][C - 1:C, :] for h in heads]
    vbs = [vs[h].astype(BF16) for h in heads]

    os_ = [_dot_nt((qs[h] * jnp.exp2(bs[h])).astype(BF16), sts[h].astype(BF16)) for h in heads]
    kds = [(kks[h] * jnp.exp2(b_lasts[h] - bs[h])).astype(BF16) for h in heads]
    st_news = [sts[h] * jnp.exp2(b_lasts[h]) + _dot_tn(vbs[h], kds[h]) for h in heads]

    bgs = [_groups(bs[h]) for h in heads]
    qgs = [_groups(qs[h]) for h in heads]
    kgs = [_groups(kks[h]) for h in heads]
    attns = [jnp.zeros((C, C), F32) for _ in heads]
    for m, pair in zip(HGRN_LEVELS, pair_masks):
        ops = [_level_operands(m, bgs[h], qgs[h], kgs[h]) for h in heads]
        for h in heads:
            attns[h] = jnp.where(pair, _dot_nt(*ops[h]), attns[h])

    for d in range(SUB):
        for h in heads:
            if d == 0:
                w = qs[h] * kks[h]
            else:
                w = jnp.concatenate(
                    [qgs[h][g] * pltpu.roll(kgs[h][g], d, axis=0)
                     * jnp.exp2(bgs[h][g] - pltpu.roll(bgs[h][g], d, axis=0))
                     for g in range(NGRP)], axis=0)
            val = jnp.sum(w, axis=1, keepdims=True)
            attns[h] = jnp.where(diag_masks[d], val, attns[h])

    outs = []
    for h in heads:
        o = os_[h] + _dot(attns[h].astype(BF16), vbs[h])
        outs.append(_rms(o, gains[h]) * _sigmoid(gates[h]))
    return outs, st_news


def _hgrn_scan_kernel(q_ref, lf_ref, v_ref, gate_ref, ng_ref, o_ref, st_scr, *, ts):
    @pl.when(pl.program_id(2) == 0)
    def _():
        st_scr[...] = jnp.zeros(st_scr.shape, F32)

    C = CHUNK
    hd = HGRN_HEAD_DIM
    lanes = [slice(h * hd, (h + 1) * hd) for h in range(HGRN_PACK)]
    gains = [ng_ref[:, sl] for sl in lanes]
    row = lax.broadcasted_iota(jnp.int32, (SUB, LANES), 0)
    trow = lax.broadcasted_iota(jnp.int32, (C, C), 0)
    scol = lax.broadcasted_iota(jnp.int32, (C, C), 1)
    pair_masks = []
    for m in HGRN_LEVELS:
        sh = m.bit_length() - 1
        pair_masks.append((((trow >> sh) & 1) == 1) & ((scol >> sh) == (trow >> sh) - 1))
    diag_masks = [(trow - scol == d) & ((trow & (SUB - 1)) >= d) for d in range(SUB)]
    masks = (row, pair_masks, diag_masks)

    def body(c, carry):
        r0 = pl.multiple_of(c * C, C)
        rows = pl.ds(r0, C)
        outs, st_news = _hgrn_chunk(
            [q_ref[0, rows, sl].astype(F32) for sl in lanes], [lf_ref[0, rows, sl] for sl in lanes],
            [v_ref[0, rows, sl] for sl in lanes], [gate_ref[0, rows, sl].astype(F32) for sl in lanes],
            gains, [st_scr[h] for h in range(HGRN_PACK)], masks)
        for h, sl in enumerate(lanes):
            st_scr[h] = st_news[h]
            o_ref[0, rows, sl] = outs[h].astype(BF16)
        return carry

    lax.fori_loop(0, ts // C, body, 0)


def _hgrn_scan(qig, lf, norm_g, ts):
    B, S, _ = lf.shape
    H = HGRN_HEADS
    hd = HGRN_HEAD_DIM
    w = HGRN_PACK * hd
    nblk = H // HGRN_PACK

    def col(section):
        return pl.BlockSpec((1, ts, w), lambda b, h, s: (b, s, h + section * nblk))

    vec = pl.BlockSpec((1, w), lambda b, h, s: (0, h))
    return pl.pallas_call(
        functools.partial(_hgrn_scan_kernel, ts=ts),
        out_shape=jax.ShapeDtypeStruct((B, S, H * hd), BF16),
        grid=(B, nblk, S // ts),
        in_specs=[col(0), col(0), col(1), col(2), vec],
        out_specs=pl.BlockSpec((1, ts, w), lambda b, h, s: (b, s, h)),
        scratch_shapes=[pltpu.VMEM((HGRN_PACK, hd, hd), F32)],
        compiler_params=_cparams("parallel", "parallel", "arbitrary"),
        name="hgrn_scan",
    )(qig, lf, qig, qig, norm_g[None, :])


def _tile(n, want):
    t = min(n, want)
    assert n % t == 0, (n, want)
    return t


def kernel(x, p, positions, norm1_g, norm2_g, mla_w_in, mla_q_norm_g, mla_w_uq, mla_kv_norm_g, mla_w_ukv, mla_w_out, conv_w_pw1, conv_b_pw1, conv_w_dw, conv_b_dw, conv_ln_g, conv_ln_b, conv_w_pw2, conv_b_pw2, hgrn_w_in, hgrn_lb_logits, hgrn_norm_g, hgrn_w_out, ffn_w_gu, ffn_w_down, ple_w_proj, ple_norm_g, ple_w_gate, final_norm_g):
    B, S, D = x.shape
    depth = norm1_g.shape[0]
    T = B * S
    tm = _tile(T, 512)
    ts_proj = _tile(S, 512)
    tq = _tile(S, 512)
    ts_conv = _tile(S, 256)
    ts_scan = _tile(S, 512)
    zero_bias = jnp.zeros((D,), F32)

    lower_bounds = _lower_bounds(hgrn_lb_logits.astype(F32))
    cos, sin = _rope_tables(positions, _tile(S, 512))

    h = x.reshape(T, D)
    p_all = p.reshape(depth * T, PLE_DIM)
    for i in range(depth):
        mixer = i % N_MIXERS
        j = i // N_MIXERS
        if mixer == 0:
            q, k, vt = _mla_proj(h.reshape(B, S, D), norm1_g[i], mla_w_in[j], mla_q_norm_g[j], mla_w_uq[j],
                                 mla_kv_norm_g[j], mla_w_ukv[j], cos, sin, ts_proj)
            a = _attention(q, k, vt, tq)
            w_o, b_o = mla_w_out[j], zero_bias
        elif mixer == 1:
            glu = _conv_glu(h, norm1_g[i], conv_w_pw1[j], conv_b_pw1[j], _tile(T, 1024), 512)
            a = _conv_dw(glu.reshape(B, S, D), conv_w_dw[j], conv_b_dw[j], conv_ln_g[j], conv_ln_b[j], ts_conv)
            w_o, b_o = conv_w_pw2[j], conv_b_pw2[j]
        else:
            qig, lf = _hgrn_proj(h, norm1_g[i], hgrn_w_in[j], lower_bounds[i:i + 1], _tile(T, 1024))
            a = _hgrn_scan(qig.reshape(B, S, 3 * D), lf.reshape(B, S, D), hgrn_norm_g[j], ts_scan)
            w_o, b_o = hgrn_w_out[j], zero_bias
        h = _layer_tail(h, a.reshape(T, D), w_o, b_o, norm2_g[i], ffn_w_gu[i], ffn_w_down[i],
                        p_all, i, ple_norm_g[i], ple_w_gate[i], ple_w_proj[i],
                        final_norm_g, i == depth - 1, tm)
    return h.reshape(B, S, D)
```

```python
import functools
import math

import jax
import jax.numpy as jnp
from jax import lax
from jax.experimental import pallas as pl
from jax.experimental.pallas import tpu as pltpu

D_MODEL = 1024
N_MIXERS = 3
PLE_DIM = 256
D_FF = 2816
MLA_HEADS = 16
MLA_NOPE = 64
MLA_ROPE = 32
MLA_V = 64
MLA_Q_LORA = 384
MLA_KV_LORA = 256
ROPE_BASE = 10000.0
CONV_WIDTH = 31
HGRN_HEADS = 8
HGRN_HEAD_DIM = 128
CHUNK = 64
EPS = 1e-6

LANES = 128
SUBLANES = 8
HEAD_PAD = 128
LOG2E = 1.4426950408889634
F32 = jnp.float32
BF16 = jnp.bfloat16
NEG_INF = float("-inf")
VMEM_LIMIT = 56 * 1024 * 1024
ROW_CHUNK = 256


def _cparams(*sem):
    return pltpu.CompilerParams(dimension_semantics=sem, vmem_limit_bytes=VMEM_LIMIT)


def _rms(x, g):
    return x * lax.rsqrt(jnp.mean(x * x, axis=-1, keepdims=True) + EPS) * g


def _dot(a, b):
    return jnp.dot(a, b, preferred_element_type=F32)


def _dot_nt(a, b):
    return lax.dot_general(a, b, (((1,), (1,)), ((), ())), preferred_element_type=F32)


def _dot_tn(a, b):
    return lax.dot_general(a, b, (((0,), (0,)), ((), ())), preferred_element_type=F32)


def _sigmoid(x):
    return 1.0 / (1.0 + jnp.exp(-x))


def _rope_table_kernel(pos_ref, invf_ref, sign_ref, cos_ref, sin_ref):
    ang = pos_ref[0].astype(F32) * invf_ref[...]
    cos_ref[0] = jnp.cos(ang)
    sin_ref[0] = jnp.sin(ang) * sign_ref[...]


def _head_lanes(nope, rope):
    ref = nope if nope is not None else rope
    half_r, half_n, gap = MLA_ROPE // 2, MLA_NOPE // 2, (HEAD_PAD - MLA_NOPE - MLA_ROPE) // 2

    def z(n):
        return jnp.zeros(ref.shape[:-1] + (n,), ref.dtype)

    n0, n1 = (nope[..., :half_n], nope[..., half_n:]) if nope is not None else (z(half_n), z(half_n))
    x1, x2 = (rope[..., :half_r], rope[..., half_r:]) if rope is not None else (z(half_r), z(half_r))
    out = jnp.concatenate([n0, x1, n1, z(gap), x2, z(gap)], axis=-1)
    assert out.shape[-1] == HEAD_PAD and half_n + half_r + half_n + gap == HEAD_PAD // 2 + half_n
    return out


def _rope_tables(positions, ts):
    B, S = positions.shape
    inv_freq = 1.0 / (ROPE_BASE ** (jnp.arange(0, MLA_ROPE, 2, dtype=F32) / MLA_ROPE))
    half = MLA_ROPE // 2
    invf = _head_lanes(None, jnp.concatenate([inv_freq, inv_freq]))[None, :]
    sign = _head_lanes(None, jnp.concatenate([-jnp.ones((half,), F32), jnp.ones((half,), F32)]))[None, :]
    vec = pl.BlockSpec((1, HEAD_PAD), lambda b, s: (0, 0))
    tab = pl.BlockSpec((1, ts, HEAD_PAD), lambda b, s: (b, s, 0))
    return pl.pallas_call(
        _rope_table_kernel,
        out_shape=(jax.ShapeDtypeStruct((B, S, HEAD_PAD), F32),) * 2,
        grid=(B, S // ts),
        in_specs=[pl.BlockSpec((1, ts, 1), lambda b, s: (b, s, 0)), vec, vec],
        out_specs=(tab, tab),
        compiler_params=_cparams("parallel", "parallel"),
        name="rope_tables",
    )(positions[:, :, None], invf, sign)


def _rope_apply(x, cos, sin_signed):
    return x * cos + pltpu.roll(x, HEAD_PAD // 2, axis=1) * sin_signed


def _mla_proj_kernel(h_ref, g1_ref, wc_ref, gq_ref, gkv_ref, wuq_ref, wuk_ref, wuv_ref,
                     cos_ref, sin_ref, q_ref, k_ref, v_ref, *, q_scale):
    for r0 in range(0, h_ref.shape[1], ROW_CHUNK):
        rows = slice(r0, r0 + ROW_CHUNK)
        xn = _rms(h_ref[0, rows, :], g1_ref[...]).astype(BF16)
        c = _dot(xn, wc_ref[...])
        cq = c[:, :MLA_Q_LORA]
        ckv = c[:, MLA_Q_LORA:MLA_Q_LORA + MLA_KV_LORA]
        kr = c[:, MLA_Q_LORA + MLA_KV_LORA:]
        cqn = _rms(cq, gq_ref[...]).astype(BF16)
        ckvn = _rms(ckv, gkv_ref[...]).astype(BF16)
        cos = cos_ref[0, rows, :]
        sin = sin_ref[0, rows, :]
        kr_rot = _rope_apply(kr, cos, sin)
        qf = _dot(cqn, wuq_ref[...])
        kf = _dot(ckvn, wuk_ref[...])
        for hh in range(MLA_HEADS):
            sl = slice(hh * HEAD_PAD, (hh + 1) * HEAD_PAD)
            q_ref[0, hh, rows, :] = (_rope_apply(qf[:, sl], cos, sin) * q_scale).astype(BF16)
            k_ref[0, hh, rows, :] = (kf[:, sl] + kr_rot).astype(BF16)
        v_ref[0, :, rows] = _dot_nt(wuv_ref[...], ckvn).astype(BF16)


def _mla_proj(h, g1, w_in, gq, w_uq, gkv, w_ukv, cos, sin, ts):
    B, S, D = h.shape
    nq, nkv = MLA_Q_LORA, MLA_KV_LORA
    w_kr = _head_lanes(None, w_in[:, nq + nkv:])
    wc = jnp.concatenate([w_in[:, :nq + nkv], w_kr], axis=1).astype(BF16)
    wuq = w_uq.reshape(nq, MLA_HEADS, MLA_NOPE + MLA_ROPE)
    wuq = _head_lanes(wuq[:, :, :MLA_NOPE], wuq[:, :, MLA_NOPE:])
    wuq = wuq.reshape(nq, MLA_HEADS * HEAD_PAD).astype(BF16)
    wukv = w_ukv.reshape(nkv, MLA_HEADS, MLA_NOPE + MLA_V)
    wuk = _head_lanes(wukv[:, :, :MLA_NOPE], None)
    wuk = wuk.reshape(nkv, MLA_HEADS * HEAD_PAD).astype(BF16)
    wuv = wukv[:, :, MLA_NOPE:].reshape(nkv, MLA_HEADS * MLA_V).T.astype(BF16)
    q_scale = (MLA_NOPE + MLA_ROPE) ** -0.5 * LOG2E

    def full(a):
        return pl.BlockSpec(a.shape, lambda b, s: (0,) * a.ndim)

    g1, gq, gkv = g1[None, :], gq[None, :], gkv[None, :]
    tab = pl.BlockSpec((1, ts, HEAD_PAD), lambda b, s: (b, s, 0))
    head_spec = pl.BlockSpec((1, MLA_HEADS, ts, HEAD_PAD), lambda b, s: (b, 0, s, 0))
    qk_shape = jax.ShapeDtypeStruct((B, MLA_HEADS, S, HEAD_PAD), BF16)
    return pl.pallas_call(
        functools.partial(_mla_proj_kernel, q_scale=q_scale),
        out_shape=(qk_shape, qk_shape, jax.ShapeDtypeStruct((B, MLA_HEADS * MLA_V, S), BF16)),
        grid=(B, S // ts),
        in_specs=[pl.BlockSpec((1, ts, D), lambda b, s: (b, s, 0)), full(g1), full(wc), full(gq),
                  full(gkv), full(wuq), full(wuk), full(wuv), tab, tab],
        out_specs=(head_spec, head_spec,
                   pl.BlockSpec((1, MLA_HEADS * MLA_V, ts), lambda b, s: (b, 0, s))),
        compiler_params=_cparams("parallel", "parallel"),
        name="mla_proj",
    )(h, g1, wc, gq, gkv, wuq, wuk, wuv, cos, sin)


ATTN_ONES_ROWS = 16
ATTN_UNROLL = 4


def _attn_kernel(q_ref, k_ref, vt_ref, o_ref, m_scr, acc_scr, sa_scr, bma_scr, sb_scr, bmb_scr, qt_scr,
                 *, tq, tk, nq):
    qi = pl.program_id(2)
    m_scr[...] = jnp.full(m_scr.shape, NEG_INF, F32)
    acc_scr[...] = jnp.zeros(acc_scr.shape, F32)
    ones_rows = jnp.ones((ATTN_ONES_ROWS, tk), BF16)
    for hh in range(2):
        qt_scr[hh] = q_ref[0, hh].T

    bufs = ((sa_scr, bma_scr), (sb_scr, bmb_scr))

    def stage(score, value):
        for hh in range(2):
            if score is not None:
                s_start, s_buf, mask_off = score
                ss_scr, sbm_scr = bufs[s_buf]
                s_start = pl.multiple_of(s_start, tk)
                if mask_off is None:
                    st = _dot(k_ref[0, hh, pl.ds(s_start, tk), :], qt_scr[hh])
                    ss_scr[hh] = st
                    sbm_scr[hh] = jnp.max(st, axis=0, keepdims=True)
                else:
                    bm = []
                    for c in range(tk // LANES):
                        lo = mask_off + c * LANES
                        st = _dot(k_ref[0, hh, pl.ds(s_start + c * LANES, LANES), :], qt_scr[hh, :, lo:])
                        key = lax.broadcasted_iota(jnp.int32, st.shape, 0)
                        qry = lax.broadcasted_iota(jnp.int32, st.shape, 1)
                        st = jnp.where(key <= qry, st, NEG_INF)
                        ss_scr[hh, c * LANES:(c + 1) * LANES, lo:] = st
                        if lo:
                            ss_scr[hh, c * LANES:(c + 1) * LANES, :lo] = jnp.full((LANES, lo), NEG_INF, F32)
                        cm = jnp.max(st, axis=0, keepdims=True)
                        bm.append(jnp.concatenate([jnp.full((1, lo), NEG_INF, F32), cm], axis=1) if lo else cm)
                    sbm_scr[hh] = functools.reduce(jnp.maximum, bm)
            if value is not None:
                v_start, v_buf = value
                vs_scr, vbm_scr = bufs[v_buf]
                m_prev = m_scr[hh]
                m_new = jnp.maximum(m_prev, vbm_scr[hh])
                alpha = jnp.exp2(m_prev - m_new)
                pt = jnp.exp2(vs_scr[hh] - m_new).astype(BF16)
                vt = vt_ref[0, hh * MLA_V:(hh + 1) * MLA_V, pl.ds(pl.multiple_of(v_start, tk), tk)]
                vt = jnp.concatenate([vt, ones_rows], axis=0)
                acc_scr[hh] = alpha * acc_scr[hh] + _dot(vt, pt)
                m_scr[hh] = m_new

    U = ATTN_UNROLL
    R = tq // tk
    q0 = qi * tq

    def first_key(n):
        return jnp.where(n < R, q0 + n * tk, (n - R) * tk)

    stage((q0, 0, 0), None)
    for n in range(R - 1):
        stage((q0 + (n + 1) * tk, (n + 1) % 2, (n + 1) * tk), (q0 + n * tk, n % 2))
    peeled = R - 1

    def run_stages(n0, count, score_last):
        for i in range(count):
            n = n0 + i
            buf = (peeled + i) % 2
            score = ((n + 1 - R) * tk, 1 - buf, None) if (i + 1 < count or score_last) else None
            stage(score, (first_key(n), buf))

    left = R * qi
    trips = left // U

    def body(t, carry):
        run_stages(peeled + U * t, U, True)
        return carry

    lax.fori_loop(0, trips, body, 0)
    for r in sorted({(R * i) % U for i in range(nq)}):
        @pl.when(left % U == r)
        def _():
            run_stages(peeled + U * trips, r + 1, False)

    ot = jnp.concatenate([acc_scr[hh, :MLA_V] / acc_scr[hh, MLA_V:MLA_V + 1] for hh in range(2)],
                         axis=0)
    o_ref[0] = ot.T.astype(BF16)


def _attention(q, k, vt, tq, tk):
    B, H, S, _ = q.shape
    return pl.pallas_call(
        functools.partial(_attn_kernel, tq=tq, tk=tk, nq=S // tq),
        out_shape=jax.ShapeDtypeStruct((B, S, H * MLA_V), BF16),
        grid=(B, H // 2, S // tq),
        in_specs=[pl.BlockSpec((1, 2, tq, HEAD_PAD), lambda b, p, i: (b, p, i, 0)),
                  pl.BlockSpec((1, 2, S, HEAD_PAD), lambda b, p, i: (b, p, 0, 0)),
                  pl.BlockSpec((1, 2 * MLA_V, S), lambda b, p, i: (b, p, 0))],
        out_specs=pl.BlockSpec((1, tq, 2 * MLA_V), lambda b, p, i: (b, i, p)),
        scratch_shapes=[pltpu.VMEM((2, 1, tq), F32),
                        pltpu.VMEM((2, MLA_V + ATTN_ONES_ROWS, tq), F32),
                        pltpu.VMEM((2, tk, tq), F32), pltpu.VMEM((2, 1, tq), F32),
                        pltpu.VMEM((2, tk, tq), F32), pltpu.VMEM((2, 1, tq), F32),
                        pltpu.VMEM((2, HEAD_PAD, tq), BF16)],
        compiler_params=_cparams("parallel", "parallel", "arbitrary"),
        name="mla_attention",
    )(q, k, vt)


HGRN_F_SECTION = 1


def _hgrn_proj_kernel(h_ref, g_ref, w_ref, lb_ref, qig_ref, lf_ref, xn_scr):
    j = pl.program_id(1)

    @pl.when(j == 0)
    def _():
        xn_scr[...] = _rms(h_ref[...], g_ref[...]).astype(BF16)

    @pl.when(j == HGRN_F_SECTION)
    def _():
        lb = lb_ref[...]
        t1 = jnp.log(lb)
        l1p = jnp.log1p(-lb)
        for r0 in range(0, lf_ref.shape[0], ROW_CHUNK):
            fz = _dot(xn_scr[r0:r0 + ROW_CHUNK, :], w_ref[...])
            log_sig = jnp.minimum(fz, 0.0) - jnp.log(1.0 + jnp.exp(-jnp.abs(fz)))
            t2 = l1p + log_sig
            lf = jnp.maximum(t1, t2) + jnp.log(1.0 + jnp.exp(-jnp.abs(t1 - t2)))
            lf_ref[r0:r0 + ROW_CHUNK, :] = lf * LOG2E

    @pl.when(j != HGRN_F_SECTION)
    def _():
        qig_ref[...] = _dot(xn_scr[...], w_ref[...]).astype(BF16)


def _hgrn_proj(h, g, w, lb, tm):
    T, D = h.shape
    assert w.shape == (D, 4 * D)

    def qig_block(i, j):
        return (i, jnp.where(j > HGRN_F_SECTION, j - 1, jnp.minimum(j, HGRN_F_SECTION - 1)))

    return pl.pallas_call(
        _hgrn_proj_kernel,
        out_shape=(jax.ShapeDtypeStruct((T, 3 * D), BF16), jax.ShapeDtypeStruct((T, D), F32)),
        grid=(T // tm, 4),
        in_specs=[pl.BlockSpec((tm, D), lambda i, j: (i, 0)), pl.BlockSpec((1, D), lambda i, j: (0, 0)),
                  pl.BlockSpec((D, D), lambda i, j: (0, j)), pl.BlockSpec((1, D), lambda i, j: (0, 0))],
        out_specs=(pl.BlockSpec((tm, D), qig_block), pl.BlockSpec((tm, D), lambda i, j: (i, 0))),
        scratch_shapes=[pltpu.VMEM((tm, D), BF16)],
        compiler_params=_cparams("parallel", "arbitrary"),
        name="hgrn_proj",
    )(h, g[None, :], w.astype(BF16), lb)


def _tail_kernel(h_ref, a_ref, wo_ref, bo_ref, g2_ref, wgu_ref, wd_ref, p_ref, gp_ref,
                 wgate_ref, wproj_ref, gf_ref, o_ref, *, final_norm):
    for r0 in range(0, o_ref.shape[0], ROW_CHUNK):
        rows = slice(r0, r0 + ROW_CHUNK)
        h1 = h_ref[rows, :] + _dot(a_ref[rows, :], wo_ref[...]) + bo_ref[...]
        xn = _rms(h1, g2_ref[...]).astype(BF16)
        gate = _dot(xn, wgu_ref[:, :D_FF])
        up = _dot(xn, wgu_ref[:, D_FF:])
        act = (gate * _sigmoid(gate) * up).astype(BF16)
        h2 = h1 + _dot(act, wd_ref[...])
        pgate = _sigmoid(_dot(_rms(h2, gp_ref[...]).astype(BF16), wgate_ref[...]))
        out = h2 + _dot(p_ref[rows, :].astype(BF16), wproj_ref[...]) * pgate
        if final_norm:
            out = _rms(out, gf_ref[...])
        o_ref[rows, :] = out


def _layer_tail(h, a, w_o, b_o, g2, w_gu, w_down, p_all, layer, g_ple, w_gate, w_proj, g_final,
                final_norm, tm):
    T, D = h.shape
    p_row0 = layer * (T // tm)

    def tile(w):
        return pl.BlockSpec((tm, w), lambda i: (i, 0))

    def resident(shape):
        return pl.BlockSpec(shape, lambda i: (0, 0), pipeline_mode=pl.Buffered(1))

    return pl.pallas_call(
        functools.partial(_tail_kernel, final_norm=final_norm),
        out_shape=jax.ShapeDtypeStruct((T, D), F32),
        grid=(T // tm,),
        in_specs=[tile(D), tile(D), resident((D, D)), resident((1, D)), resident((1, D)),
                  resident((D, 2 * D_FF)), resident((D_FF, D)),
                  pl.BlockSpec((tm, PLE_DIM), lambda i: (p_row0 + i, 0)),
                  resident((1, D)), resident((D, D)), resident((PLE_DIM, D)), resident((1, D))],
        out_specs=tile(D),
        compiler_params=_cparams("parallel"),
        name="layer_tail",
    )(h, a, w_o.astype(BF16), b_o[None, :], g2[None, :], w_gu.astype(BF16), w_down.astype(BF16), p_all,
      g_ple[None, :], w_gate.astype(BF16), w_proj.astype(BF16), g_final[None, :])


def _glu_kernel(h_ref, g_ref, wa_ref, wb_ref, ba_ref, bb_ref, o_ref, xn_scr):
    @pl.when(pl.program_id(1) == 0)
    def _():
        xn_scr[...] = _rms(h_ref[...], g_ref[...]).astype(BF16)

    for r0 in range(0, o_ref.shape[0], ROW_CHUNK):
        xn = xn_scr[r0:r0 + ROW_CHUNK, :]
        a = _dot(xn, wa_ref[...]) + ba_ref[...]
        b = _dot(xn, wb_ref[...]) + bb_ref[...]
        o_ref[r0:r0 + ROW_CHUNK, :] = a * _sigmoid(b)


def _conv_glu(h, g, w_pw1, b_pw1, tm, tn):
    T, D = h.shape
    nn = D // tn
    w = w_pw1.astype(BF16)
    b = b_pw1[None, :]
    return pl.pallas_call(
        _glu_kernel,
        out_shape=jax.ShapeDtypeStruct((T, D), F32),
        grid=(T // tm, nn),
        in_specs=[pl.BlockSpec((tm, D), lambda i, j: (i, 0)), pl.BlockSpec((1, D), lambda i, j: (0, 0)),
                  pl.BlockSpec((D, tn), lambda i, j: (0, j)), pl.BlockSpec((D, tn), lambda i, j: (0, j + nn)),
                  pl.BlockSpec((1, tn), lambda i, j: (0, j)), pl.BlockSpec((1, tn), lambda i, j: (0, j + nn))],
        out_specs=pl.BlockSpec((tm, tn), lambda i, j: (i, j)),
        scratch_shapes=[pltpu.VMEM((tm, D), BF16)],
        compiler_params=_cparams("parallel", "arbitrary"),
        name="conv_glu",
    )(h, g[None, :], w, w, b, b)


CONV_HALO = 32
CONV_ROWS = 32


def _conv_dw_kernel(a_ref, wdw_ref, bdw_ref, lng_ref, lnb_ref, z_ref, ext_scr, y_scr, sh_scr, *, ts):
    @pl.when(pl.program_id(1) == 0)
    def _():
        ext_scr[0:CONV_HALO, :] = jnp.zeros((CONV_HALO, D_MODEL), F32)

    ext_scr[CONV_HALO:, :] = a_ref[0]
    off = CONV_HALO - (CONV_WIDTH - 1)
    rows_sh = sh_scr.shape[1]
    for ph in range(1, SUBLANES):
        sh_scr[ph - 1] = ext_scr[ph:ph + rows_sh, :]
    for c0 in range(0, D_MODEL, LANES):
        cols = slice(c0, c0 + LANES)
        wtap = [jnp.broadcast_to(wdw_ref[j:j + 1, cols], (SUBLANES, LANES)) for j in range(CONV_WIDTH)]
        for r0 in range(0, ts, CONV_ROWS):
            groups = range(CONV_ROWS // SUBLANES)
            acc = [jnp.zeros((SUBLANES, LANES), F32) for _ in groups]
            for j in range(CONV_WIDTH):
                grp, ph = divmod(off + j, SUBLANES)
                src = ext_scr if ph == 0 else sh_scr.at[ph - 1]
                for g in groups:
                    start = r0 + (grp + g) * SUBLANES
                    acc[g] = acc[g] + wtap[j] * src[start:start + SUBLANES, cols]
            for g in groups:
                y_scr[r0 + g * SUBLANES:r0 + (g + 1) * SUBLANES, cols] = acc[g]
    ext_scr[0:CONV_HALO, :] = ext_scr[ts:ts + CONV_HALO, :]
    y = y_scr[...] + bdw_ref[...]
    mu = jnp.mean(y, axis=-1, keepdims=True)
    yc = y - mu
    var = jnp.mean(yc * yc, axis=-1, keepdims=True)
    z = yc * lax.rsqrt(var + EPS) * lng_ref[...] + lnb_ref[...]
    z_ref[0] = (z * _sigmoid(z)).astype(BF16)


def _conv_dw(a, w_dw, b_dw, ln_g, ln_b, ts):
    B, S, D = a.shape

    def full(x):
        return pl.BlockSpec(x.shape, lambda b, s: (0,) * x.ndim)

    tile = pl.BlockSpec((1, ts, D), lambda b, s: (b, s, 0))
    args = (w_dw, b_dw[None, :], ln_g[None, :], ln_b[None, :])
    return pl.pallas_call(
        functools.partial(_conv_dw_kernel, ts=ts),
        out_shape=jax.ShapeDtypeStruct((B, S, D), BF16),
        grid=(B, S // ts),
        in_specs=[tile] + [full(x) for x in args],
        out_specs=tile,
        scratch_shapes=[pltpu.VMEM((ts + CONV_HALO, D), F32), pltpu.VMEM((ts, D), F32),
                        pltpu.VMEM((SUBLANES - 1, ts + CONV_HALO - SUBLANES, D), F32)],
        compiler_params=_cparams("parallel", "arbitrary"),
        name="conv_depthwise",
    )(a, *args)


def _lower_bounds_kernel(x_ref, o_ref):
    x = x_ref[...]
    n = x.shape[0]
    m = x[0:1]
    for r in range(1, n):
        m = jnp.maximum(m, x[r:r + 1])
    e = jnp.exp(x - m)
    tot = e[0:1]
    for r in range(1, n):
        tot = tot + e[r:r + 1]
    sm = e / tot
    cum = sm[0:1]
    first = cum
    o_ref[0:1, :] = cum - first
    for r in range(1, n):
        cum = cum + sm[r:r + 1]
        o_ref[r:r + 1, :] = cum - first


def _lower_bounds(logits):
    return pl.pallas_call(
        _lower_bounds_kernel,
        out_shape=jax.ShapeDtypeStruct(logits.shape, F32),
        name="hgrn_lower_bounds",
    )(logits)


SUB = SUBLANES
NGRP = CHUNK // SUB
HGRN_LEVELS = (32, 16, 8, 4, 2)
HGRN_DIRECT = 2
HGRN_PACK = 8


def _groups(x):
    return [x[g * SUB:(g + 1) * SUB, :] for g in range(NGRP)]


def _cumsum_chunk(x, row):
    gs = _groups(x)
    for step in (1, 2, 4):
        gs = [g + jnp.where(row >= step, pltpu.roll(g, step, axis=0), 0.0) for g in gs]
    out = [gs[0]]
    run = gs[0][SUB - 1:SUB, :]
    for g in range(1, NGRP):
        out.append(gs[g] + run)
        run = run + gs[g][SUB - 1:SUB, :]
    return jnp.concatenate(out, axis=0)


def _level_operands(m, bg, qg, kg, row):
    zeros = jnp.zeros((SUB, LANES), F32)
    qs, ks = [], []
    for g in range(NGRP):
        if m >= SUB:
            blk = (g * SUB) // m
            if blk % 2 == 1:
                ref = bg[blk * m // SUB - 1][SUB - 1:SUB, :]
                qs.append(qg[g] * jnp.exp2(bg[g] - ref))
                ks.append(zeros)
            else:
                ref = bg[(blk + 1) * m // SUB - 1][SUB - 1:SUB, :]
                qs.append(zeros)
                ks.append(kg[g] * jnp.exp2(ref - bg[g]))
        else:
            b = bg[g]
            if 2 * m == SUB:
                ref = b[m - 1:m, :]
            else:
                ref = jnp.where(row >= SUB // 2, b[SUB // 2 + m - 1:SUB // 2 + m, :], b[m - 1:m, :])
            odd = (row & m) != 0
            t = b - ref
            e = jnp.exp2(jnp.where(odd, t, -t))
            qs.append(jnp.where(odd, qg[g] * e, 0.0))
            ks.append(jnp.where(odd, 0.0, kg[g] * e))
    return jnp.concatenate(qs, axis=0).astype(BF16), jnp.concatenate(ks, axis=0).astype(BF16)


def _hgrn_chunk(qs, lfs, vs, gates, gains, sts, masks):
    C = CHUNK
    row, pair_masks, diag_masks = masks
    heads = range(len(qs))
    fs = [jnp.exp2(lfs[h]) for h in heads]
    kks = [1.0 - fs[h] for h in heads]
    bs = [_cumsum_chunk(lfs[h], row) for h in heads]
    b_lasts = [bs[h][C - 1:C, :] for h in heads]
    vbs = [vs[h].astype(BF16) for h in heads]

    os_ = [_dot_nt((qs[h] * jnp.exp2(bs[h])).astype(BF16), sts[h].astype(BF16)) for h in heads]
    kds = [(kks[h] * jnp.exp2(b_lasts[h] - bs[h])).astype(BF16) for h in heads]
    st_news = [sts[h] * jnp.exp2(b_lasts[h]) + _dot_tn(vbs[h], kds[h]) for h in heads]

    bgs = [_groups(bs[h]) for h in heads]
    qgs = [_groups(qs[h]) for h in heads]
    kgs = [_groups(kks[h]) for h in heads]
    fgs = [_groups(fs[h]) for h in heads]
    attns = [jnp.zeros((C, C), F32) for _ in heads]
    for m, pair in zip(HGRN_LEVELS, pair_masks):
        ops = [_level_operands(m, bgs[h], qgs[h], kgs[h], row) for h in heads]
        for h in heads:
            attns[h] = jnp.where(pair, _dot_nt(*ops[h]), attns[h])

    for d in range(HGRN_DIRECT):
        for h in heads:
            if d == 0:
                w = qs[h] * kks[h]
            else:
                w = jnp.concatenate(
                    [qgs[h][g] * fgs[h][g] * pltpu.roll(kgs[h][g], d, axis=0)
                     for g in range(NGRP)], axis=0)
            val = jnp.sum(w, axis=1, keepdims=True)
            attns[h] = jnp.where(diag_masks[d], val, attns[h])

    outs = []
    for h in heads:
        o = os_[h] + _dot(attns[h].astype(BF16), vbs[h])
        outs.append(_rms(o, gains[h]) * _sigmoid(gates[h]))
    return outs, st_news


def _hgrn_scan_kernel(q_ref, lf_ref, v_ref, gate_ref, ng_ref, o_ref, st_scr, *, ts):
    @pl.when(pl.program_id(2) == 0)
    def _():
        st_scr[...] = jnp.zeros(st_scr.shape, F32)

    C = CHUNK
    hd = HGRN_HEAD_DIM
    lanes = [slice(h * hd, (h + 1) * hd) for h in range(HGRN_PACK)]
    gains = [ng_ref[:, sl] for sl in lanes]
    row = lax.broadcasted_iota(jnp.int32, (SUB, LANES), 0)
    trow = lax.broadcasted_iota(jnp.int32, (C, C), 0)
    scol = lax.broadcasted_iota(jnp.int32, (C, C), 1)
    pair_masks = []
    for m in HGRN_LEVELS:
        sh = m.bit_length() - 1
        pair_masks.append((((trow >> sh) & 1) == 1) & ((scol >> sh) == (trow >> sh) - 1))
    diag_masks = [(trow - scol == d) & ((trow & (HGRN_DIRECT - 1)) >= d) for d in range(HGRN_DIRECT)]
    masks = (row, pair_masks, diag_masks)

    def body(c, carry):
        r0 = pl.multiple_of(c * C, C)
        rows = pl.ds(r0, C)
        outs, st_news = _hgrn_chunk(
            [q_ref[0, rows, sl].astype(F32) for sl in lanes], [lf_ref[0, rows, sl] for sl in lanes],
            [v_ref[0, rows, sl] for sl in lanes], [gate_ref[0, rows, sl].astype(F32) for sl in lanes],
            gains, [st_scr[h] for h in range(HGRN_PACK)], masks)
        for h, sl in enumerate(lanes):
            st_scr[h] = st_news[h]
            o_ref[0, rows, sl] = outs[h].astype(BF16)
        return carry

    lax.fori_loop(0, ts // C, body, 0)


def _hgrn_scan(qig, lf, norm_g, ts):
    B, S, _ = lf.shape
    H = HGRN_HEADS
    hd = HGRN_HEAD_DIM
    w = HGRN_PACK * hd
    nblk = H // HGRN_PACK

    def col(section):
        return pl.BlockSpec((1, ts, w), lambda b, h, s: (b, s, h + section * nblk))

    vec = pl.BlockSpec((1, w), lambda b, h, s: (0, h))
    return pl.pallas_call(
        functools.partial(_hgrn_scan_kernel, ts=ts),
        out_shape=jax.ShapeDtypeStruct((B, S, H * hd), BF16),
        grid=(B, nblk, S // ts),
        in_specs=[col(0), col(0), col(1), col(2), vec],
        out_specs=pl.BlockSpec((1, ts, w), lambda b, h, s: (b, s, h)),
        scratch_shapes=[pltpu.VMEM((HGRN_PACK, hd, hd), F32)],
        compiler_params=_cparams("parallel", "parallel", "arbitrary"),
        name="hgrn_scan",
    )(qig, lf, qig, qig, norm_g[None, :])


def _tile(n, want):
    t = min(n, want)
    assert n % t == 0, (n, want)
    return t


def kernel(x, p, positions, norm1_g, norm2_g, mla_w_in, mla_q_norm_g, mla_w_uq, mla_kv_norm_g, mla_w_ukv, mla_w_out, conv_w_pw1, conv_b_pw1, conv_w_dw, conv_b_dw, conv_ln_g, conv_ln_b, conv_w_pw2, conv_b_pw2, hgrn_w_in, hgrn_lb_logits, hgrn_norm_g, hgrn_w_out, ffn_w_gu, ffn_w_down, ple_w_proj, ple_norm_g, ple_w_gate, final_norm_g):
    B, S, D = x.shape
    depth = norm1_g.shape[0]
    T = B * S
    tm = _tile(T, 512)
    ts_proj = _tile(S, 512)
    tq = _tile(S, 512)
    ts_conv = _tile(S, 256)
    ts_scan = _tile(S, 512)
    zero_bias = jnp.zeros((D,), F32)

    lower_bounds = _lower_bounds(hgrn_lb_logits.astype(F32))
    cos, sin = _rope_tables(positions, _tile(S, 512))

    h = x.reshape(T, D)
    p_all = p.reshape(depth * T, PLE_DIM)
    for i in range(depth):
        mixer = i % N_MIXERS
        j = i // N_MIXERS
        if mixer == 0:
            q, k, vt = _mla_proj(h.reshape(B, S, D), norm1_g[i], mla_w_in[j], mla_q_norm_g[j], mla_w_uq[j],
                                 mla_kv_norm_g[j], mla_w_ukv[j], cos, sin, ts_proj)
            a = _attention(q, k, vt, tq, tq)
            w_o, b_o = mla_w_out[j], zero_bias
        elif mixer == 1:
            glu = _conv_glu(h, norm1_g[i], conv_w_pw1[j], conv_b_pw1[j], _tile(T, 1024), 512)
            a = _conv_dw(glu.reshape(B, S, D), conv_w_dw[j], conv_b_dw[j], conv_ln_g[j], conv_ln_b[j], ts_conv)
            w_o, b_o = conv_w_pw2[j], conv_b_pw2[j]
        else:
            qig, lf = _hgrn_proj(h, norm1_g[i], hgrn_w_in[j], lower_bounds[i:i + 1], _tile(T, 1024))
            a = _hgrn_scan(qig.reshape(B, S, 3 * D), lf.reshape(B, S, D), hgrn_norm_g[j], ts_scan)
            w_o, b_o = hgrn_w_out[j], zero_bias
        h = _layer_tail(h, a.reshape(T, D), w_o, b_o, norm2_g[i], ffn_w_gu[i], ffn_w_down[i],
                        p_all, i, ple_norm_g[i], ple_w_gate[i], ple_w_proj[i],
                        final_norm_g, i == depth - 1, _tile(T, 1024))
    return h.reshape(B, S, D)
```

```python
import functools

import jax
import jax.numpy as jnp
from jax import lax
from jax.experimental import pallas as pl
from jax.experimental.pallas import tpu as pltpu

D_MODEL = 1024
N_MIXERS = 3
PLE_DIM = 256
D_FF = 2816
MLA_HEADS = 16
MLA_NOPE = 64
MLA_ROPE = 32
MLA_V = 64
MLA_Q_LORA = 384
MLA_KV_LORA = 256
ROPE_BASE = 10000.0
CONV_WIDTH = 31
HGRN_HEADS = 8
HGRN_HEAD_DIM = 128
CHUNK = 64
EPS = 1e-6

LANES = 128
SUBLANES = 8
HEAD_PAD = 128
LOG2E = 1.4426950408889634
F32 = jnp.float32
BF16 = jnp.bfloat16
NEG_INF = float("-inf")
VMEM_LIMIT = 56 * 1024 * 1024
ROW_CHUNK = 256


def _cparams(*sem):
    return pltpu.CompilerParams(dimension_semantics=sem, vmem_limit_bytes=VMEM_LIMIT)


def _rms(x, g):
    return x * lax.rsqrt(jnp.mean(x * x, axis=-1, keepdims=True) + EPS) * g


def _dot(a, b):
    return jnp.dot(a, b, preferred_element_type=F32)


def _dot_nt(a, b):
    return lax.dot_general(a, b, (((1,), (1,)), ((), ())), preferred_element_type=F32)


def _dot_tn(a, b):
    return lax.dot_general(a, b, (((0,), (0,)), ((), ())), preferred_element_type=F32)


def _sigmoid(x):
    return 1.0 / (1.0 + jnp.exp(-x))


def _rope_table_kernel(pos_ref, invf_ref, sign_ref, cos_ref, sin_ref):
    ang = pos_ref[0].astype(F32) * invf_ref[...]
    cos_ref[0] = jnp.cos(ang)
    sin_ref[0] = jnp.sin(ang) * sign_ref[...]


def _head_lanes(nope, rope):
    ref = nope if nope is not None else rope
    half_r, half_n, gap = MLA_ROPE // 2, MLA_NOPE // 2, (HEAD_PAD - MLA_NOPE - MLA_ROPE) // 2

    def z(n):
        return jnp.zeros(ref.shape[:-1] + (n,), ref.dtype)

    n0, n1 = (nope[..., :half_n], nope[..., half_n:]) if nope is not None else (z(half_n), z(half_n))
    x1, x2 = (rope[..., :half_r], rope[..., half_r:]) if rope is not None else (z(half_r), z(half_r))
    out = jnp.concatenate([n0, x1, n1, z(gap), x2, z(gap)], axis=-1)
    assert out.shape[-1] == HEAD_PAD and half_n + half_r + half_n + gap == HEAD_PAD // 2 + half_n
    return out


def _rope_tables(positions, ts):
    B, S = positions.shape
    inv_freq = 1.0 / (ROPE_BASE ** (jnp.arange(0, MLA_ROPE, 2, dtype=F32) / MLA_ROPE))
    half = MLA_ROPE // 2
    invf = _head_lanes(None, jnp.concatenate([inv_freq, inv_freq]))[None, :]
    sign = _head_lanes(None, jnp.concatenate([-jnp.ones((half,), F32), jnp.ones((half,), F32)]))[None, :]
    vec = pl.BlockSpec((1, HEAD_PAD), lambda b, s: (0, 0))
    tab = pl.BlockSpec((1, ts, HEAD_PAD), lambda b, s: (b, s, 0))
    return pl.pallas_call(
        _rope_table_kernel,
        out_shape=(jax.ShapeDtypeStruct((B, S, HEAD_PAD), F32),) * 2,
        grid=(B, S // ts),
        in_specs=[pl.BlockSpec((1, ts, 1), lambda b, s: (b, s, 0)), vec, vec],
        out_specs=(tab, tab),
        compiler_params=_cparams("parallel", "parallel"),
        name="rope_tables",
    )(positions[:, :, None], invf, sign)


def _rope_apply(x, cos, sin_signed):
    return x * cos + pltpu.roll(x, HEAD_PAD // 2, axis=1) * sin_signed


def _mla_proj_kernel(h_ref, g1_ref, wc_ref, gq_ref, gkv_ref, wuq_ref, wuk_ref, wuv_ref,
                     cos_ref, sin_ref, q_ref, k_ref, v_ref, *, q_scale):
    for r0 in range(0, h_ref.shape[1], ROW_CHUNK):
        rows = slice(r0, r0 + ROW_CHUNK)
        xn = _rms(h_ref[0, rows, :], g1_ref[...]).astype(BF16)
        c = _dot(xn, wc_ref[...])
        cq = c[:, :MLA_Q_LORA]
        ckv = c[:, MLA_Q_LORA:MLA_Q_LORA + MLA_KV_LORA]
        kr = c[:, MLA_Q_LORA + MLA_KV_LORA:]
        cqn = _rms(cq, gq_ref[...]).astype(BF16)
        ckvn = _rms(ckv, gkv_ref[...]).astype(BF16)
        cos = cos_ref[0, rows, :]
        sin = sin_ref[0, rows, :]
        kr_rot = _rope_apply(kr, cos, sin)
        qf = _dot(cqn, wuq_ref[...])
        kf = _dot(ckvn, wuk_ref[...])
        for hh in range(MLA_HEADS):
            sl = slice(hh * HEAD_PAD, (hh + 1) * HEAD_PAD)
            q_ref[0, hh, rows, :] = (_rope_apply(qf[:, sl], cos, sin) * q_scale).astype(BF16)
            k_ref[0, hh, rows, :] = (kf[:, sl] + kr_rot).astype(BF16)
        v_ref[0, :, rows] = _dot_nt(wuv_ref[...], ckvn).astype(BF16)


def _mla_proj(h, g1, w_in, gq, w_uq, gkv, w_ukv, cos, sin, ts):
    B, S, D = h.shape
    nq, nkv = MLA_Q_LORA, MLA_KV_LORA
    w_kr = _head_lanes(None, w_in[:, nq + nkv:])
    wc = jnp.concatenate([w_in[:, :nq + nkv], w_kr], axis=1).astype(BF16)
    wuq = w_uq.reshape(nq, MLA_HEADS, MLA_NOPE + MLA_ROPE)
    wuq = _head_lanes(wuq[:, :, :MLA_NOPE], wuq[:, :, MLA_NOPE:])
    wuq = wuq.reshape(nq, MLA_HEADS * HEAD_PAD).astype(BF16)
    wukv = w_ukv.reshape(nkv, MLA_HEADS, MLA_NOPE + MLA_V)
    wuk = _head_lanes(wukv[:, :, :MLA_NOPE], None)
    wuk = wuk.reshape(nkv, MLA_HEADS * HEAD_PAD).astype(BF16)
    wuv = wukv[:, :, MLA_NOPE:].reshape(nkv, MLA_HEADS * MLA_V).T.astype(BF16)
    q_scale = (MLA_NOPE + MLA_ROPE) ** -0.5 * LOG2E

    def full(a):
        return pl.BlockSpec(a.shape, lambda b, s: (0,) * a.ndim)

    g1, gq, gkv = g1[None, :], gq[None, :], gkv[None, :]
    tab = pl.BlockSpec((1, ts, HEAD_PAD), lambda b, s: (b, s, 0))
    head_spec = pl.BlockSpec((1, MLA_HEADS, ts, HEAD_PAD), lambda b, s: (b, 0, s, 0))
    qk_shape = jax.ShapeDtypeStruct((B, MLA_HEADS, S, HEAD_PAD), BF16)
    return pl.pallas_call(
        functools.partial(_mla_proj_kernel, q_scale=q_scale),
        out_shape=(qk_shape, qk_shape, jax.ShapeDtypeStruct((B, MLA_HEADS * MLA_V, S), BF16)),
        grid=(B, S // ts),
        in_specs=[pl.BlockSpec((1, ts, D), lambda b, s: (b, s, 0)), full(g1), full(wc), full(gq),
                  full(gkv), full(wuq), full(wuk), full(wuv), tab, tab],
        out_specs=(head_spec, head_spec,
                   pl.BlockSpec((1, MLA_HEADS * MLA_V, ts), lambda b, s: (b, 0, s))),
        compiler_params=_cparams("parallel", "parallel"),
        name="mla_proj",
    )(h, g1, wc, gq, gkv, wuq, wuk, wuv, cos, sin)


ATTN_ONES_ROWS = 16
ATTN_UNROLL = 4


def _attn_kernel(q_ref, k_ref, vt_ref, o_ref, m_scr, acc_scr, sa_scr, bma_scr, sb_scr, bmb_scr, qt_scr,
                 *, tq, tk, nq):
    qi = pl.program_id(2)
    m_scr[...] = jnp.full(m_scr.shape, NEG_INF, F32)
    acc_scr[...] = jnp.zeros(acc_scr.shape, F32)
    ones_rows = jnp.ones((ATTN_ONES_ROWS, tk), BF16)
    for hh in range(2):
        qt_scr[hh] = q_ref[0, hh].T

    bufs = ((sa_scr, bma_scr), (sb_scr, bmb_scr))

    def stage(score, value):
        for hh in range(2):
            if score is not None:
                s_start, s_buf, mask_off = score
                ss_scr, sbm_scr = bufs[s_buf]
                s_start = pl.multiple_of(s_start, tk)
                if mask_off is None:
                    st = _dot(k_ref[0, hh, pl.ds(s_start, tk), :], qt_scr[hh])
                    ss_scr[hh] = st
                    sbm_scr[hh] = jnp.max(st, axis=0, keepdims=True)
                else:
                    bm = []
                    for c in range(tk // LANES):
                        lo = mask_off + c * LANES
                        st = _dot(k_ref[0, hh, pl.ds(s_start + c * LANES, LANES), :], qt_scr[hh, :, lo:])
                        key = lax.broadcasted_iota(jnp.int32, st.shape, 0)
                        qry = lax.broadcasted_iota(jnp.int32, st.shape, 1)
                        st = jnp.where(key <= qry, st, NEG_INF)
                        ss_scr[hh, c * LANES:(c + 1) * LANES, lo:] = st
                        if lo:
                            ss_scr[hh, c * LANES:(c + 1) * LANES, :lo] = jnp.full((LANES, lo), NEG_INF, F32)
                        cm = jnp.max(st, axis=0, keepdims=True)
                        bm.append(jnp.concatenate([jnp.full((1, lo), NEG_INF, F32), cm], axis=1) if lo else cm)
                    sbm_scr[hh] = functools.reduce(jnp.maximum, bm)
            if value is not None:
                v_start, v_buf = value
                vs_scr, vbm_scr = bufs[v_buf]
                m_prev = m_scr[hh]
                m_new = jnp.maximum(m_prev, vbm_scr[hh])
                alpha = jnp.exp2(m_prev - m_new)
                pt = jnp.exp2(vs_scr[hh] - m_new).astype(BF16)
                vt = vt_ref[0, hh * MLA_V:(hh + 1) * MLA_V, pl.ds(pl.multiple_of(v_start, tk), tk)]
                vt = jnp.concatenate([vt, ones_rows], axis=0)
                acc_scr[hh] = alpha * acc_scr[hh] + _dot(vt, pt)
                m_scr[hh] = m_new

    U = ATTN_UNROLL
    R = tq // tk
    q0 = qi * tq

    def first_key(n):
        return jnp.where(n < R, q0 + n * tk, (n - R) * tk)

    stage((q0, 0, 0), None)
    for n in range(R - 1):
        stage((q0 + (n + 1) * tk, (n + 1) % 2, (n + 1) * tk), (q0 + n * tk, n % 2))
    peeled = R - 1

    def run_stages(n0, count, score_last):
        for i in range(count):
            n = n0 + i
            buf = (peeled + i) % 2
            score = ((n + 1 - R) * tk, 1 - buf, None) if (i + 1 < count or score_last) else None
            stage(score, (first_key(n), buf))

    left = R * qi
    trips = left // U

    def body(t, carry):
        run_stages(peeled + U * t, U, True)
        return carry

    lax.fori_loop(0, trips, body, 0)
    for r in sorted({(R * i) % U for i in range(nq)}):
        @pl.when(left % U == r)
        def _():
            run_stages(peeled + U * trips, r + 1, False)

    ot = jnp.concatenate([acc_scr[hh, :MLA_V] / acc_scr[hh, MLA_V:MLA_V + 1] for hh in range(2)],
                         axis=0)
    o_ref[0] = ot.T.astype(BF16)


def _attention(q, k, vt, tq, tk):
    B, H, S, _ = q.shape
    return pl.pallas_call(
        functools.partial(_attn_kernel, tq=tq, tk=tk, nq=S // tq),
        out_shape=jax.ShapeDtypeStruct((B, S, H * MLA_V), BF16),
        grid=(B, H // 2, S // tq),
        in_specs=[pl.BlockSpec((1, 2, tq, HEAD_PAD), lambda b, p, i: (b, p, i, 0)),
                  pl.BlockSpec((1, 2, S, HEAD_PAD), lambda b, p, i: (b, p, 0, 0)),
                  pl.BlockSpec((1, 2 * MLA_V, S), lambda b, p, i: (b, p, 0))],
        out_specs=pl.BlockSpec((1, tq, 2 * MLA_V), lambda b, p, i: (b, i, p)),
        scratch_shapes=[pltpu.VMEM((2, 1, tq), F32),
                        pltpu.VMEM((2, MLA_V + ATTN_ONES_ROWS, tq), F32),
                        pltpu.VMEM((2, tk, tq), F32), pltpu.VMEM((2, 1, tq), F32),
                        pltpu.VMEM((2, tk, tq), F32), pltpu.VMEM((2, 1, tq), F32),
                        pltpu.VMEM((2, HEAD_PAD, tq), BF16)],
        compiler_params=_cparams("parallel", "parallel", "arbitrary"),
        name="mla_attention",
    )(q, k, vt)


HGRN_F_SECTION = 1


def _hgrn_proj_kernel(h_ref, g_ref, w_ref, lb_ref, qig_ref, lf_ref, xn_scr):
    j = pl.program_id(1)

    @pl.when(j == 0)
    def _():
        xn_scr[...] = _rms(h_ref[...], g_ref[...]).astype(BF16)

    @pl.when(j == HGRN_F_SECTION)
    def _():
        lb = lb_ref[...]
        t1 = jnp.log(lb)
        l1p = jnp.log1p(-lb)
        for r0 in range(0, lf_ref.shape[0], ROW_CHUNK):
            fz = _dot(xn_scr[r0:r0 + ROW_CHUNK, :], w_ref[...])
            log_sig = jnp.minimum(fz, 0.0) - jnp.log(1.0 + jnp.exp(-jnp.abs(fz)))
            t2 = l1p + log_sig
            lf = jnp.maximum(t1, t2) + jnp.log(1.0 + jnp.exp(-jnp.abs(t1 - t2)))
            lf_ref[r0:r0 + ROW_CHUNK, :] = lf * LOG2E

    @pl.when(j != HGRN_F_SECTION)
    def _():
        qig_ref[...] = _dot(xn_scr[...], w_ref[...]).astype(BF16)


def _hgrn_proj(h, g, w, lb, tm):
    T, D = h.shape
    assert w.shape == (D, 4 * D)

    def qig_block(i, j):
        return (i, jnp.where(j > HGRN_F_SECTION, j - 1, jnp.minimum(j, HGRN_F_SECTION - 1)))

    return pl.pallas_call(
        _hgrn_proj_kernel,
        out_shape=(jax.ShapeDtypeStruct((T, 3 * D), BF16), jax.ShapeDtypeStruct((T, D), F32)),
        grid=(T // tm, 4),
        in_specs=[pl.BlockSpec((tm, D), lambda i, j: (i, 0)), pl.BlockSpec((1, D), lambda i, j: (0, 0)),
                  pl.BlockSpec((D, D), lambda i, j: (0, j)), pl.BlockSpec((1, D), lambda i, j: (0, 0))],
        out_specs=(pl.BlockSpec((tm, D), qig_block), pl.BlockSpec((tm, D), lambda i, j: (i, 0))),
        scratch_shapes=[pltpu.VMEM((tm, D), BF16)],
        compiler_params=_cparams("parallel", "arbitrary"),
        name="hgrn_proj",
    )(h, g[None, :], w.astype(BF16), lb)


def _tail_kernel(h_ref, a_ref, wo_ref, bo_ref, g2_ref, wgu_ref, wd_ref, p_ref, gp_ref,
                 wgate_ref, wproj_ref, gf_ref, o_ref, *, final_norm):
    for r0 in range(0, o_ref.shape[0], ROW_CHUNK):
        rows = slice(r0, r0 + ROW_CHUNK)
        h1 = h_ref[rows, :] + _dot(a_ref[rows, :], wo_ref[...]) + bo_ref[...]
        xn = _rms(h1, g2_ref[...]).astype(BF16)
        gate = _dot(xn, wgu_ref[:, :D_FF])
        up = _dot(xn, wgu_ref[:, D_FF:])
        act = (gate * _sigmoid(gate) * up).astype(BF16)
        h2 = h1 + _dot(act, wd_ref[...])
        pgate = _sigmoid(_dot(_rms(h2, gp_ref[...]).astype(BF16), wgate_ref[...]))
        out = h2 + _dot(p_ref[rows, :].astype(BF16), wproj_ref[...]) * pgate
        if final_norm:
            out = _rms(out, gf_ref[...])
        o_ref[rows, :] = out


def _layer_tail(h, a, w_o, b_o, g2, w_gu, w_down, p_all, layer, g_ple, w_gate, w_proj, g_final,
                final_norm, tm):
    T, D = h.shape
    p_row0 = layer * (T // tm)

    def tile(w):
        return pl.BlockSpec((tm, w), lambda i: (i, 0))

    def resident(shape):
        return pl.BlockSpec(shape, lambda i: (0, 0), pipeline_mode=pl.Buffered(1))

    return pl.pallas_call(
        functools.partial(_tail_kernel, final_norm=final_norm),
        out_shape=jax.ShapeDtypeStruct((T, D), F32),
        grid=(T // tm,),
        in_specs=[tile(D), tile(D), resident((D, D)), resident((1, D)), resident((1, D)),
                  resident((D, 2 * D_FF)), resident((D_FF, D)),
                  pl.BlockSpec((tm, PLE_DIM), lambda i: (p_row0 + i, 0)),
                  resident((1, D)), resident((D, D)), resident((PLE_DIM, D)), resident((1, D))],
        out_specs=tile(D),
        compiler_params=_cparams("parallel"),
        name="layer_tail",
    )(h, a, w_o.astype(BF16), b_o[None, :], g2[None, :], w_gu.astype(BF16), w_down.astype(BF16), p_all,
      g_ple[None, :], w_gate.astype(BF16), w_proj.astype(BF16), g_final[None, :])


def _glu_kernel(h_ref, g_ref, wa_ref, wb_ref, ba_ref, bb_ref, o_ref, xn_scr):
    @pl.when(pl.program_id(1) == 0)
    def _():
        xn_scr[...] = _rms(h_ref[...], g_ref[...]).astype(BF16)

    for r0 in range(0, o_ref.shape[0], ROW_CHUNK):
        xn = xn_scr[r0:r0 + ROW_CHUNK, :]
        a = _dot(xn, wa_ref[...]) + ba_ref[...]
        b = _dot(xn, wb_ref[...]) + bb_ref[...]
        o_ref[r0:r0 + ROW_CHUNK, :] = a * _sigmoid(b)


def _conv_glu(h, g, w_pw1, b_pw1, tm, tn):
    T, D = h.shape
    nn = D // tn
    w = w_pw1.astype(BF16)
    b = b_pw1[None, :]
    return pl.pallas_call(
        _glu_kernel,
        out_shape=jax.ShapeDtypeStruct((T, D), F32),
        grid=(T // tm, nn),
        in_specs=[pl.BlockSpec((tm, D), lambda i, j: (i, 0)), pl.BlockSpec((1, D), lambda i, j: (0, 0)),
                  pl.BlockSpec((D, tn), lambda i, j: (0, j)), pl.BlockSpec((D, tn), lambda i, j: (0, j + nn)),
                  pl.BlockSpec((1, tn), lambda i, j: (0, j)), pl.BlockSpec((1, tn), lambda i, j: (0, j + nn))],
        out_specs=pl.BlockSpec((tm, tn), lambda i, j: (i, j)),
        scratch_shapes=[pltpu.VMEM((tm, D), BF16)],
        compiler_params=_cparams("parallel", "arbitrary"),
        name="conv_glu",
    )(h, g[None, :], w, w, b, b)


CONV_HALO = 32
CONV_ROWS = 64


def _conv_dw_kernel(a_ref, wdw_ref, bdw_ref, lng_ref, lnb_ref, z_ref, ext_scr, y_scr, sh_scr, *, ts):
    @pl.when(pl.program_id(1) == 0)
    def _():
        ext_scr[0:CONV_HALO, :] = jnp.zeros((CONV_HALO, D_MODEL), F32)

    ext_scr[CONV_HALO:, :] = a_ref[0]
    off = CONV_HALO - (CONV_WIDTH - 1)
    rows_sh = sh_scr.shape[1]
    for ph in range(1, SUBLANES):
        sh_scr[ph - 1] = ext_scr[ph:ph + rows_sh, :]
    for c0 in range(0, D_MODEL, LANES):
        cols = slice(c0, c0 + LANES)
        wtap = [jnp.broadcast_to(wdw_ref[j:j + 1, cols], (SUBLANES, LANES)) for j in range(CONV_WIDTH)]
        def row_block(rb, carry, cols=cols, wtap=wtap):
            r0 = pl.multiple_of(rb * CONV_ROWS, CONV_ROWS)
            groups = range(CONV_ROWS // SUBLANES)
            acc = [jnp.zeros((SUBLANES, LANES), F32) for _ in groups]
            for j in range(CONV_WIDTH):
                grp, ph = divmod(off + j, SUBLANES)
                src = ext_scr if ph == 0 else sh_scr.at[ph - 1]
                for g in groups:
                    acc[g] = acc[g] + wtap[j] * src[pl.ds(r0 + (grp + g) * SUBLANES, SUBLANES), cols]
            for g in groups:
                y_scr[pl.ds(r0 + g * SUBLANES, SUBLANES), cols] = acc[g]
            return carry

        lax.fori_loop(0, ts // CONV_ROWS, row_block, 0)
    ext_scr[0:CONV_HALO, :] = ext_scr[ts:ts + CONV_HALO, :]
    y = y_scr[...] + bdw_ref[...]
    mu = jnp.mean(y, axis=-1, keepdims=True)
    yc = y - mu
    var = jnp.mean(yc * yc, axis=-1, keepdims=True)
    z = yc * lax.rsqrt(var + EPS) * lng_ref[...] + lnb_ref[...]
    z_ref[0] = (z * _sigmoid(z)).astype(BF16)


def _conv_dw(a, w_dw, b_dw, ln_g, ln_b, ts):
    B, S, D = a.shape

    def full(x):
        return pl.BlockSpec(x.shape, lambda b, s: (0,) * x.ndim)

    tile = pl.BlockSpec((1, ts, D), lambda b, s: (b, s, 0))
    args = (w_dw, b_dw[None, :], ln_g[None, :], ln_b[None, :])
    return pl.pallas_call(
        functools.partial(_conv_dw_kernel, ts=ts),
        out_shape=jax.ShapeDtypeStruct((B, S, D), BF16),
        grid=(B, S // ts),
        in_specs=[tile] + [full(x) for x in args],
        out_specs=tile,
        scratch_shapes=[pltpu.VMEM((ts + CONV_HALO, D), F32), pltpu.VMEM((ts, D), F32),
                        pltpu.VMEM((SUBLANES - 1, ts + CONV_HALO - SUBLANES, D), F32)],
        compiler_params=_cparams("parallel", "arbitrary"),
        name="conv_depthwise",
    )(a, *args)


def _lower_bounds_kernel(x_ref, o_ref):
    x = x_ref[...]
    n = x.shape[0]
    m = x[0:1]
    for r in range(1, n):
        m = jnp.maximum(m, x[r:r + 1])
    e = jnp.exp(x - m)
    tot = e[0:1]
    for r in range(1, n):
        tot = tot + e[r:r + 1]
    sm = e / tot
    cum = sm[0:1]
    first = cum
    o_ref[0:1, :] = cum - first
    for r in range(1, n):
        cum = cum + sm[r:r + 1]
        o_ref[r:r + 1, :] = cum - first


def _lower_bounds(logits):
    return pl.pallas_call(
        _lower_bounds_kernel,
        out_shape=jax.ShapeDtypeStruct(logits.shape, F32),
        name="hgrn_lower_bounds",
    )(logits)


SUB = SUBLANES
NGRP = CHUNK // SUB
HGRN_LEVELS = (32, 16, 8, 4, 2)
HGRN_DIRECT = 2
HGRN_PACK = 8


def _groups(x):
    return [x[g * SUB:(g + 1) * SUB, :] for g in range(NGRP)]


def _cumsum_chunk(x, row):
    gs = _groups(x)
    for step in (1, 2, 4):
        gs = [g + jnp.where(row >= step, pltpu.roll(g, step, axis=0), 0.0) for g in gs]
    out = [gs[0]]
    run = gs[0][SUB - 1:SUB, :]
    for g in range(1, NGRP):
        out.append(gs[g] + run)
        run = run + gs[g][SUB - 1:SUB, :]
    return jnp.concatenate(out, axis=0)


def _level_operands(m, bg, qg, kg, row):
    zeros = jnp.zeros((SUB, LANES), F32)
    qs, ks = [], []
    for g in range(NGRP):
        if m >= SUB:
            blk = (g * SUB) // m
            if blk % 2 == 1:
                ref = bg[blk * m // SUB - 1][SUB - 1:SUB, :]
                qs.append(qg[g] * jnp.exp2(bg[g] - ref))
                ks.append(zeros)
            else:
                ref = bg[(blk + 1) * m // SUB - 1][SUB - 1:SUB, :]
                qs.append(zeros)
                ks.append(kg[g] * jnp.exp2(ref - bg[g]))
        else:
            b = bg[g]
            if 2 * m == SUB:
                ref = b[m - 1:m, :]
            else:
                ref = jnp.where(row >= SUB // 2, b[SUB // 2 + m - 1:SUB // 2 + m, :], b[m - 1:m, :])
            odd = (row & m) != 0
            t = b - ref
            e = jnp.exp2(jnp.where(odd, t, -t))
            qs.append(jnp.where(odd, qg[g] * e, 0.0))
            ks.append(jnp.where(odd, 0.0, kg[g] * e))
    return jnp.concatenate(qs, axis=0).astype(BF16), jnp.concatenate(ks, axis=0).astype(BF16)


def _hgrn_chunk(qs, lfs, vs, gates, gains, sts, masks):
    C = CHUNK
    row, pair_masks, diag_masks = masks
    heads = range(len(qs))
    fs = [jnp.exp2(lfs[h]) for h in heads]
    kks = [1.0 - fs[h] for h in heads]
    bs = [_cumsum_chunk(lfs[h], row) for h in heads]
    b_lasts = [bs[h][C - 1:C, :] for h in heads]
    vbs = [vs[h].astype(BF16) for h in heads]

    os_ = [_dot_nt((qs[h] * jnp.exp2(bs[h])).astype(BF16), sts[h].astype(BF16)) for h in heads]
    kds = [(kks[h] * jnp.exp2(b_lasts[h] - bs[h])).astype(BF16) for h in heads]
    st_news = [sts[h] * jnp.exp2(b_lasts[h]) + _dot_tn(vbs[h], kds[h]) for h in heads]

    bgs = [_groups(bs[h]) for h in heads]
    qgs = [_groups(qs[h]) for h in heads]
    kgs = [_groups(kks[h]) for h in heads]
    fgs = [_groups(fs[h]) for h in heads]
    attns = [jnp.zeros((C, C), F32) for _ in heads]
    for m, pair in zip(HGRN_LEVELS, pair_masks):
        ops = [_level_operands(m, bgs[h], qgs[h], kgs[h], row) for h in heads]
        for h in heads:
            attns[h] = jnp.where(pair, _dot_nt(*ops[h]), attns[h])

    for d in range(HGRN_DIRECT):
        for h in heads:
            if d == 0:
                w = qs[h] * kks[h]
            else:
                w = jnp.concatenate(
                    [qgs[h][g] * fgs[h][g] * pltpu.roll(kgs[h][g], d, axis=0)
                     for g in range(NGRP)], axis=0)
            val = jnp.sum(w, axis=1, keepdims=True)
            attns[h] = jnp.where(diag_masks[d], val, attns[h])

    outs = []
    for h in heads:
        o = os_[h] + _dot(attns[h].astype(BF16), vbs[h])
        outs.append(_rms(o, gains[h]) * _sigmoid(gates[h]))
    return outs, st_news


def _hgrn_scan_kernel(q_ref, lf_ref, v_ref, gate_ref, ng_ref, o_ref, st_scr, *, ts):
    @pl.when(pl.program_id(2) == 0)
    def _():
        st_scr[...] = jnp.zeros(st_scr.shape, F32)

    C = CHUNK
    hd = HGRN_HEAD_DIM
    lanes = [slice(h * hd, (h + 1) * hd) for h in range(HGRN_PACK)]
    gains = [ng_ref[:, sl] for sl in lanes]
    row = lax.broadcasted_iota(jnp.int32, (SUB, LANES), 0)
    trow = lax.broadcasted_iota(jnp.int32, (C, C), 0)
    scol = lax.broadcasted_iota(jnp.int32, (C, C), 1)
    pair_masks = []
    for m in HGRN_LEVELS:
        sh = m.bit_length() - 1
        pair_masks.append((((trow >> sh) & 1) == 1) & ((scol >> sh) == (trow >> sh) - 1))
    diag_masks = [(trow - scol == d) & ((trow & (HGRN_DIRECT - 1)) >= d) for d in range(HGRN_DIRECT)]
    masks = (row, pair_masks, diag_masks)

    def body(c, carry):
        r0 = pl.multiple_of(c * C, C)
        rows = pl.ds(r0, C)
        outs, st_news = _hgrn_chunk(
            [q_ref[0, rows, sl].astype(F32) for sl in lanes], [lf_ref[0, rows, sl] for sl in lanes],
            [v_ref[0, rows, sl] for sl in lanes], [gate_ref[0, rows, sl].astype(F32) for sl in lanes],
            gains, [st_scr[h] for h in range(HGRN_PACK)], masks)
        for h, sl in enumerate(lanes):
            st_scr[h] = st_news[h]
            o_ref[0, rows, sl] = outs[h].astype(BF16)
        return carry

    lax.fori_loop(0, ts // C, body, 0)


def _hgrn_scan(qig, lf, norm_g, ts):
    B, S, _ = lf.shape
    H = HGRN_HEADS
    hd = HGRN_HEAD_DIM
    w = HGRN_PACK * hd
    nblk = H // HGRN_PACK

    def col(section):
        return pl.BlockSpec((1, ts, w), lambda b, h, s: (b, s, h + section * nblk))

    vec = pl.BlockSpec((1, w), lambda b, h, s: (0, h))
    return pl.pallas_call(
        functools.partial(_hgrn_scan_kernel, ts=ts),
        out_shape=jax.ShapeDtypeStruct((B, S, H * hd), BF16),
        grid=(B, nblk, S // ts),
        in_specs=[col(0), col(0), col(1), col(2), vec],
        out_specs=pl.BlockSpec((1, ts, w), lambda b, h, s: (b, s, h)),
        scratch_shapes=[pltpu.VMEM((HGRN_PACK, hd, hd), F32)],
        compiler_params=_cparams("parallel", "parallel", "arbitrary"),
        name="hgrn_scan",
    )(qig, lf, qig, qig, norm_g[None, :])


def _tile(n, want):
    t = min(n, want)
    assert n % t == 0, (n, want)
    return t


def kernel(x, p, positions, norm1_g, norm2_g, mla_w_in, mla_q_norm_g, mla_w_uq, mla_kv_norm_g, mla_w_ukv, mla_w_out, conv_w_pw1, conv_b_pw1, conv_w_dw, conv_b_dw, conv_ln_g, conv_ln_b, conv_w_pw2, conv_b_pw2, hgrn_w_in, hgrn_lb_logits, hgrn_norm_g, hgrn_w_out, ffn_w_gu, ffn_w_down, ple_w_proj, ple_norm_g, ple_w_gate, final_norm_g):
    B, S, D = x.shape
    depth = norm1_g.shape[0]
    T = B * S
    tm = _tile(T, 512)
    ts_proj = _tile(S, 512)
    tq = _tile(S, 512)
    ts_conv = _tile(S, 256)
    ts_scan = _tile(S, 512)
    zero_bias = jnp.zeros((D,), F32)

    lower_bounds = _lower_bounds(hgrn_lb_logits.astype(F32))
    cos, sin = _rope_tables(positions, _tile(S, 512))

    h = x.reshape(T, D)
    p_all = p.reshape(depth * T, PLE_DIM)
    for i in range(depth):
        mixer = i % N_MIXERS
        j = i // N_MIXERS
        if mixer == 0:
            q, k, vt = _mla_proj(h.reshape(B, S, D), norm1_g[i], mla_w_in[j], mla_q_norm_g[j], mla_w_uq[j],
                                 mla_kv_norm_g[j], mla_w_ukv[j], cos, sin, ts_proj)
            a = _attention(q, k, vt, tq, tq)
            w_o, b_o = mla_w_out[j], zero_bias
        elif mixer == 1:
            glu = _conv_glu(h, norm1_g[i], conv_w_pw1[j], conv_b_pw1[j], _tile(T, 1024), 512)
            a = _conv_dw(glu.reshape(B, S, D), conv_w_dw[j], conv_b_dw[j], conv_ln_g[j], conv_ln_b[j], ts_conv)
            w_o, b_o = conv_w_pw2[j], conv_b_pw2[j]
        else:
            qig, lf = _hgrn_proj(h, norm1_g[i], hgrn_w_in[j], lower_bounds[i:i + 1], _tile(T, 1024))
            a = _hgrn_scan(qig.reshape(B, S, 3 * D), lf.reshape(B, S, D), hgrn_norm_g[j], ts_scan)
            w_o, b_o = hgrn_w_out[j], zero_bias
        h = _layer_tail(h, a.reshape(T, D), w_o, b_o, norm2_g[i], ffn_w_gu[i], ffn_w_down[i],
                        p_all, i, ple_norm_g[i], ple_w_gate[i], ple_w_proj[i],
                        final_norm_g, i == depth - 1, _tile(T, 1024))
    return h.reshape(B, S, D)
```

```python
import functools

import jax
import jax.numpy as jnp
from jax import lax
from jax.experimental import pallas as pl
from jax.experimental.pallas import tpu as pltpu

D_MODEL = 1024
N_MIXERS = 3
PLE_DIM = 256
D_FF = 2816
MLA_HEADS = 16
MLA_NOPE = 64
MLA_ROPE = 32
MLA_V = 64
MLA_Q_LORA = 384
MLA_KV_LORA = 256
ROPE_BASE = 10000.0
CONV_WIDTH = 31
HGRN_HEADS = 8
HGRN_HEAD_DIM = 128
CHUNK = 64
EPS = 1e-6

LANES = 128
SUBLANES = 8
HEAD_PAD = 128
LOG2E = 1.4426950408889634
F32 = jnp.float32
BF16 = jnp.bfloat16
NEG_INF = float("-inf")
VMEM_LIMIT = 56 * 1024 * 1024
ROW_CHUNK = 256


def _cparams(*sem):
    return pltpu.CompilerParams(dimension_semantics=sem, vmem_limit_bytes=VMEM_LIMIT)


def _rms(x, g):
    return x * lax.rsqrt(jnp.mean(x * x, axis=-1, keepdims=True) + EPS) * g


def _dot(a, b):
    return jnp.dot(a, b, preferred_element_type=F32)


def _dot_nt(a, b):
    return lax.dot_general(a, b, (((1,), (1,)), ((), ())), preferred_element_type=F32)


def _dot_tn(a, b):
    return lax.dot_general(a, b, (((0,), (0,)), ((), ())), preferred_element_type=F32)


def _sigmoid(x):
    return 1.0 / (1.0 + jnp.exp(-x))


def _rope_table_kernel(pos_ref, invf_ref, sign_ref, cos_ref, sin_ref):
    ang = pos_ref[0].astype(F32) * invf_ref[...]
    cos_ref[0] = jnp.cos(ang)
    sin_ref[0] = jnp.sin(ang) * sign_ref[...]


def _head_lanes(nope, rope):
    ref = nope if nope is not None else rope
    half_r, half_n, gap = MLA_ROPE // 2, MLA_NOPE // 2, (HEAD_PAD - MLA_NOPE - MLA_ROPE) // 2

    def z(n):
        return jnp.zeros(ref.shape[:-1] + (n,), ref.dtype)

    n0, n1 = (nope[..., :half_n], nope[..., half_n:]) if nope is not None else (z(half_n), z(half_n))
    x1, x2 = (rope[..., :half_r], rope[..., half_r:]) if rope is not None else (z(half_r), z(half_r))
    out = jnp.concatenate([n0, x1, n1, z(gap), x2, z(gap)], axis=-1)
    assert out.shape[-1] == HEAD_PAD and half_n + half_r + half_n + gap == HEAD_PAD // 2 + half_n
    return out


def _rope_tables(positions, ts):
    B, S = positions.shape
    inv_freq = 1.0 / (ROPE_BASE ** (jnp.arange(0, MLA_ROPE, 2, dtype=F32) / MLA_ROPE))
    half = MLA_ROPE // 2
    invf = _head_lanes(None, jnp.concatenate([inv_freq, inv_freq]))[None, :]
    sign = _head_lanes(None, jnp.concatenate([-jnp.ones((half,), F32), jnp.ones((half,), F32)]))[None, :]
    vec = pl.BlockSpec((1, HEAD_PAD), lambda b, s: (0, 0))
    tab = pl.BlockSpec((1, ts, HEAD_PAD), lambda b, s: (b, s, 0))
    return pl.pallas_call(
        _rope_table_kernel,
        out_shape=(jax.ShapeDtypeStruct((B, S, HEAD_PAD), F32),) * 2,
        grid=(B, S // ts),
        in_specs=[pl.BlockSpec((1, ts, 1), lambda b, s: (b, s, 0)), vec, vec],
        out_specs=(tab, tab),
        compiler_params=_cparams("parallel", "parallel"),
        name="rope_tables",
    )(positions[:, :, None], invf, sign)


def _rope_apply(x, cos, sin_signed):
    return x * cos + pltpu.roll(x, HEAD_PAD // 2, axis=1) * sin_signed


def _mla_proj_kernel(h_ref, g1_ref, wc_ref, gq_ref, gkv_ref, wuq_ref, wuk_ref, wuv_ref,
                     cos_ref, sin_ref, q_ref, k_ref, v_ref, *, q_scale):
    for r0 in range(0, h_ref.shape[1], ROW_CHUNK):
        rows = slice(r0, r0 + ROW_CHUNK)
        xn = _rms(h_ref[0, rows, :], g1_ref[...]).astype(BF16)
        c = _dot(xn, wc_ref[...])
        cq = c[:, :MLA_Q_LORA]
        ckv = c[:, MLA_Q_LORA:MLA_Q_LORA + MLA_KV_LORA]
        kr = c[:, MLA_Q_LORA + MLA_KV_LORA:]
        cqn = _rms(cq, gq_ref[...]).astype(BF16)
        ckvn = _rms(ckv, gkv_ref[...]).astype(BF16)
        cos = cos_ref[0, rows, :]
        sin = sin_ref[0, rows, :]
        kr_rot = _rope_apply(kr, cos, sin)
        qf = _dot(cqn, wuq_ref[...])
        kf = _dot(ckvn, wuk_ref[...])
        for hh in range(MLA_HEADS):
            sl = slice(hh * HEAD_PAD, (hh + 1) * HEAD_PAD)
            q_ref[0, hh, rows, :] = (_rope_apply(qf[:, sl], cos, sin) * q_scale).astype(BF16)
            k_ref[0, hh, rows, :] = (kf[:, sl] + kr_rot).astype(BF16)
        v_ref[0, :, rows] = _dot_nt(wuv_ref[...], ckvn).astype(BF16)


def _mla_proj(h, g1, w_in, gq, w_uq, gkv, w_ukv, cos, sin, ts):
    B, S, D = h.shape
    nq, nkv = MLA_Q_LORA, MLA_KV_LORA
    w_kr = _head_lanes(None, w_in[:, nq + nkv:])
    wc = jnp.concatenate([w_in[:, :nq + nkv], w_kr], axis=1).astype(BF16)
    wuq = w_uq.reshape(nq, MLA_HEADS, MLA_NOPE + MLA_ROPE)
    wuq = _head_lanes(wuq[:, :, :MLA_NOPE], wuq[:, :, MLA_NOPE:])
    wuq = wuq.reshape(nq, MLA_HEADS * HEAD_PAD).astype(BF16)
    wukv = w_ukv.reshape(nkv, MLA_HEADS, MLA_NOPE + MLA_V)
    wuk = _head_lanes(wukv[:, :, :MLA_NOPE], None)
    wuk = wuk.reshape(nkv, MLA_HEADS * HEAD_PAD).astype(BF16)
    wuv = wukv[:, :, MLA_NOPE:].reshape(nkv, MLA_HEADS * MLA_V).T.astype(BF16)
    q_scale = (MLA_NOPE + MLA_ROPE) ** -0.5 * LOG2E

    def full(a):
        return pl.BlockSpec(a.shape, lambda b, s: (0,) * a.ndim)

    g1, gq, gkv = g1[None, :], gq[None, :], gkv[None, :]
    tab = pl.BlockSpec((1, ts, HEAD_PAD), lambda b, s: (b, s, 0))
    head_spec = pl.BlockSpec((1, MLA_HEADS, ts, HEAD_PAD), lambda b, s: (b, 0, s, 0))
    qk_shape = jax.ShapeDtypeStruct((B, MLA_HEADS, S, HEAD_PAD), BF16)
    return pl.pallas_call(
        functools.partial(_mla_proj_kernel, q_scale=q_scale),
        out_shape=(qk_shape, qk_shape, jax.ShapeDtypeStruct((B, MLA_HEADS * MLA_V, S), BF16)),
        grid=(B, S // ts),
        in_specs=[pl.BlockSpec((1, ts, D), lambda b, s: (b, s, 0)), full(g1), full(wc), full(gq),
                  full(gkv), full(wuq), full(wuk), full(wuv), tab, tab],
        out_specs=(head_spec, head_spec,
                   pl.BlockSpec((1, MLA_HEADS * MLA_V, ts), lambda b, s: (b, 0, s))),
        compiler_params=_cparams("parallel", "parallel"),
        name="mla_proj",
    )(h, g1, wc, gq, gkv, wuq, wuk, wuv, cos, sin)


ATTN_ONES_ROWS = 16
ATTN_UNROLL = 4


def _attn_kernel(q_ref, k_ref, vt_ref, o_ref, m_scr, acc_scr, sa_scr, bma_scr, sb_scr, bmb_scr, qt_scr,
                 *, tq, tk, nq):
    qi = pl.program_id(2)
    m_scr[...] = jnp.full(m_scr.shape, NEG_INF, F32)
    acc_scr[...] = jnp.zeros(acc_scr.shape, F32)
    ones_rows = jnp.ones((ATTN_ONES_ROWS, tk), BF16)
    for hh in range(2):
        qt_scr[hh] = q_ref[0, hh].T

    bufs = ((sa_scr, bma_scr), (sb_scr, bmb_scr))

    def stage(score, value):
        for hh in range(2):
            if score is not None:
                s_start, s_buf, mask_off = score
                ss_scr, sbm_scr = bufs[s_buf]
                s_start = pl.multiple_of(s_start, tk)
                if mask_off is None:
                    st = _dot(k_ref[0, hh, pl.ds(s_start, tk), :], qt_scr[hh])
                    ss_scr[hh] = st
                    sbm_scr[hh] = jnp.max(st, axis=0, keepdims=True)
                else:
                    bm = []
                    for c in range(tk // LANES):
                        lo = mask_off + c * LANES
                        st = _dot(k_ref[0, hh, pl.ds(s_start + c * LANES, LANES), :], qt_scr[hh, :, lo:])
                        key = lax.broadcasted_iota(jnp.int32, st.shape, 0)
                        qry = lax.broadcasted_iota(jnp.int32, st.shape, 1)
                        st = jnp.where(key <= qry, st, NEG_INF)
                        ss_scr[hh, c * LANES:(c + 1) * LANES, lo:] = st
                        if lo:
                            ss_scr[hh, c * LANES:(c + 1) * LANES, :lo] = jnp.full((LANES, lo), NEG_INF, F32)
                        cm = jnp.max(st, axis=0, keepdims=True)
                        bm.append(jnp.concatenate([jnp.full((1, lo), NEG_INF, F32), cm], axis=1) if lo else cm)
                    sbm_scr[hh] = functools.reduce(jnp.maximum, bm)
            if value is not None:
                v_start, v_buf = value
                vs_scr, vbm_scr = bufs[v_buf]
                m_prev = m_scr[hh]
                m_new = jnp.maximum(m_prev, vbm_scr[hh])
                alpha = jnp.exp2(m_prev - m_new)
                pt = jnp.exp2(vs_scr[hh] - m_new).astype(BF16)
                vt = vt_ref[0, hh * MLA_V:(hh + 1) * MLA_V, pl.ds(pl.multiple_of(v_start, tk), tk)]
                vt = jnp.concatenate([vt, ones_rows], axis=0)
                acc_scr[hh] = alpha * acc_scr[hh] + _dot(vt, pt)
                m_scr[hh] = m_new

    U = ATTN_UNROLL
    R = tq // tk
    q0 = qi * tq

    def first_key(n):
        return jnp.where(n < R, q0 + n * tk, (n - R) * tk)

    stage((q0, 0, 0), None)
    for n in range(R - 1):
        stage((q0 + (n + 1) * tk, (n + 1) % 2, (n + 1) * tk), (q0 + n * tk, n % 2))
    peeled = R - 1

    def run_stages(n0, count, score_last):
        for i in range(count):
            n = n0 + i
            buf = (peeled + i) % 2
            score = ((n + 1 - R) * tk, 1 - buf, None) if (i + 1 < count or score_last) else None
            stage(score, (first_key(n), buf))

    left = R * qi
    trips = left // U

    def body(t, carry):
        run_stages(peeled + U * t, U, True)
        return carry

    lax.fori_loop(0, trips, body, 0)
    for r in sorted({(R * i) % U for i in range(nq)}):
        @pl.when(left % U == r)
        def _():
            run_stages(peeled + U * trips, r + 1, False)

    ot = jnp.concatenate([acc_scr[hh, :MLA_V] / acc_scr[hh, MLA_V:MLA_V + 1] for hh in range(2)],
                         axis=0)
    o_ref[0] = ot.T.astype(BF16)


def _attention(q, k, vt, tq, tk):
    B, H, S, _ = q.shape
    return pl.pallas_call(
        functools.partial(_attn_kernel, tq=tq, tk=tk, nq=S // tq),
        out_shape=jax.ShapeDtypeStruct((B, S, H * MLA_V), BF16),
        grid=(B, H // 2, S // tq),
        in_specs=[pl.BlockSpec((1, 2, tq, HEAD_PAD), lambda b, p, i: (b, p, i, 0)),
                  pl.BlockSpec((1, 2, S, HEAD_PAD), lambda b, p, i: (b, p, 0, 0)),
                  pl.BlockSpec((1, 2 * MLA_V, S), lambda b, p, i: (b, p, 0))],
        out_specs=pl.BlockSpec((1, tq, 2 * MLA_V), lambda b, p, i: (b, i, p)),
        scratch_shapes=[pltpu.VMEM((2, 1, tq), F32),
                        pltpu.VMEM((2, MLA_V + ATTN_ONES_ROWS, tq), F32),
                        pltpu.VMEM((2, tk, tq), F32), pltpu.VMEM((2, 1, tq), F32),
                        pltpu.VMEM((2, tk, tq), F32), pltpu.VMEM((2, 1, tq), F32),
                        pltpu.VMEM((2, HEAD_PAD, tq), BF16)],
        compiler_params=_cparams("parallel", "parallel", "arbitrary"),
        name="mla_attention",
    )(q, k, vt)


HGRN_F_SECTION = 1


def _hgrn_proj_kernel(h_ref, g_ref, w_ref, lb_ref, qig_ref, lf_ref, xn_scr):
    j = pl.program_id(1)

    @pl.when(j == 0)
    def _():
        xn_scr[...] = _rms(h_ref[...], g_ref[...]).astype(BF16)

    @pl.when(j == HGRN_F_SECTION)
    def _():
        lb = lb_ref[...]
        t1 = jnp.log(lb)
        l1p = jnp.log1p(-lb)
        for r0 in range(0, lf_ref.shape[0], ROW_CHUNK):
            fz = _dot(xn_scr[r0:r0 + ROW_CHUNK, :], w_ref[...])
            log_sig = jnp.minimum(fz, 0.0) - jnp.log(1.0 + jnp.exp(-jnp.abs(fz)))
            t2 = l1p + log_sig
            lf = jnp.maximum(t1, t2) + jnp.log(1.0 + jnp.exp(-jnp.abs(t1 - t2)))
            lf_ref[r0:r0 + ROW_CHUNK, :] = lf * LOG2E

    @pl.when(j != HGRN_F_SECTION)
    def _():
        qig_ref[...] = _dot(xn_scr[...], w_ref[...]).astype(BF16)


def _hgrn_proj(h, g, w, lb, tm):
    T, D = h.shape
    assert w.shape == (D, 4 * D)

    def qig_block(i, j):
        return (i, jnp.where(j > HGRN_F_SECTION, j - 1, jnp.minimum(j, HGRN_F_SECTION - 1)))

    return pl.pallas_call(
        _hgrn_proj_kernel,
        out_shape=(jax.ShapeDtypeStruct((T, 3 * D), BF16), jax.ShapeDtypeStruct((T, D), F32)),
        grid=(T // tm, 4),
        in_specs=[pl.BlockSpec((tm, D), lambda i, j: (i, 0)), pl.BlockSpec((1, D), lambda i, j: (0, 0)),
                  pl.BlockSpec((D, D), lambda i, j: (0, j)), pl.BlockSpec((1, D), lambda i, j: (0, 0))],
        out_specs=(pl.BlockSpec((tm, D), qig_block), pl.BlockSpec((tm, D), lambda i, j: (i, 0))),
        scratch_shapes=[pltpu.VMEM((tm, D), BF16)],
        compiler_params=_cparams("parallel", "arbitrary"),
        name="hgrn_proj",
    )(h, g[None, :], w.astype(BF16), lb)


def _tail_kernel(h_ref, a_ref, wo_ref, bo_ref, g2_ref, wgu_ref, wd_ref, p_ref, gp_ref,
                 wgate_ref, wproj_ref, gf_ref, o_ref, *, final_norm):
    for r0 in range(0, o_ref.shape[0], ROW_CHUNK):
        rows = slice(r0, r0 + ROW_CHUNK)
        h1 = h_ref[rows, :] + _dot(a_ref[rows, :], wo_ref[...]) + bo_ref[...]
        xn = _rms(h1, g2_ref[...]).astype(BF16)
        gate = _dot(xn, wgu_ref[:, :D_FF])
        up = _dot(xn, wgu_ref[:, D_FF:])
        act = (gate * _sigmoid(gate) * up).astype(BF16)
        h2 = h1 + _dot(act, wd_ref[...])
        pgate = _sigmoid(_dot(_rms(h2, gp_ref[...]).astype(BF16), wgate_ref[...]))
        out = h2 + _dot(p_ref[rows, :].astype(BF16), wproj_ref[...]) * pgate
        if final_norm:
            out = _rms(out, gf_ref[...])
        o_ref[rows, :] = out


def _layer_tail(h, a, w_o, b_o, g2, w_gu, w_down, p_all, layer, g_ple, w_gate, w_proj, g_final,
                final_norm, tm):
    T, D = h.shape
    p_row0 = layer * (T // tm)

    def tile(w):
        return pl.BlockSpec((tm, w), lambda i: (i, 0))

    def resident(shape):
        return pl.BlockSpec(shape, lambda i: (0, 0), pipeline_mode=pl.Buffered(1))

    return pl.pallas_call(
        functools.partial(_tail_kernel, final_norm=final_norm),
        out_shape=jax.ShapeDtypeStruct((T, D), F32),
        grid=(T // tm,),
        in_specs=[tile(D), tile(D), resident((D, D)), resident((1, D)), resident((1, D)),
                  resident((D, 2 * D_FF)), resident((D_FF, D)),
                  pl.BlockSpec((tm, PLE_DIM), lambda i: (p_row0 + i, 0)),
                  resident((1, D)), resident((D, D)), resident((PLE_DIM, D)), resident((1, D))],
        out_specs=tile(D),
        compiler_params=_cparams("parallel"),
        name="layer_tail",
    )(h, a, w_o.astype(BF16), b_o[None, :], g2[None, :], w_gu.astype(BF16), w_down.astype(BF16), p_all,
      g_ple[None, :], w_gate.astype(BF16), w_proj.astype(BF16), g_final[None, :])


def _glu_kernel(h_ref, g_ref, wa_ref, wb_ref, ba_ref, bb_ref, o_ref, xn_scr):
    @pl.when(pl.program_id(1) == 0)
    def _():
        xn_scr[...] = _rms(h_ref[...], g_ref[...]).astype(BF16)

    for r0 in range(0, o_ref.shape[0], ROW_CHUNK):
        xn = xn_scr[r0:r0 + ROW_CHUNK, :]
        a = _dot(xn, wa_ref[...]) + ba_ref[...]
        b = _dot(xn, wb_ref[...]) + bb_ref[...]
        o_ref[r0:r0 + ROW_CHUNK, :] = a * _sigmoid(b)


def _conv_glu(h, g, w_pw1, b_pw1, tm, tn):
    T, D = h.shape
    nn = D // tn
    w = w_pw1.astype(BF16)
    b = b_pw1[None, :]
    return pl.pallas_call(
        _glu_kernel,
        out_shape=jax.ShapeDtypeStruct((T, D), F32),
        grid=(T // tm, nn),
        in_specs=[pl.BlockSpec((tm, D), lambda i, j: (i, 0)), pl.BlockSpec((1, D), lambda i, j: (0, 0)),
                  pl.BlockSpec((D, tn), lambda i, j: (0, j)), pl.BlockSpec((D, tn), lambda i, j: (0, j + nn)),
                  pl.BlockSpec((1, tn), lambda i, j: (0, j)), pl.BlockSpec((1, tn), lambda i, j: (0, j + nn))],
        out_specs=pl.BlockSpec((tm, tn), lambda i, j: (i, j)),
        scratch_shapes=[pltpu.VMEM((tm, D), BF16)],
        compiler_params=_cparams("parallel", "arbitrary"),
        name="conv_glu",
    )(h, g[None, :], w, w, b, b)


CONV_HALO = 32
CONV_ROWS = 64


def _conv_dw_kernel(a_ref, wdw_ref, bdw_ref, lng_ref, lnb_ref, z_ref, ext_scr, y_scr, sh_scr, *, ts):
    @pl.when(pl.program_id(1) == 0)
    def _():
        ext_scr[0:CONV_HALO, :] = jnp.zeros((CONV_HALO, D_MODEL), F32)

    ext_scr[CONV_HALO:, :] = a_ref[0]
    off = CONV_HALO - (CONV_WIDTH - 1)
    rows_sh = sh_scr.shape[1]
    for ph in range(1, SUBLANES):
        sh_scr[ph - 1] = ext_scr[ph:ph + rows_sh, :]
    for c0 in range(0, D_MODEL, LANES):
        cols = slice(c0, c0 + LANES)
        wtap = [jnp.broadcast_to(wdw_ref[j:j + 1, cols], (SUBLANES, LANES)) for j in range(CONV_WIDTH)]
        def row_block(rb, carry, cols=cols, wtap=wtap):
            r0 = pl.multiple_of(rb * CONV_ROWS, CONV_ROWS)
            groups = range(CONV_ROWS // SUBLANES)
            acc = [jnp.zeros((SUBLANES, LANES), F32) for _ in groups]
            for j in range(CONV_WIDTH):
                grp, ph = divmod(off + j, SUBLANES)
                src = ext_scr if ph == 0 else sh_scr.at[ph - 1]
                for g in groups:
                    acc[g] = acc[g] + wtap[j] * src[pl.ds(r0 + (grp + g) * SUBLANES, SUBLANES), cols]
            for g in groups:
                y_scr[pl.ds(r0 + g * SUBLANES, SUBLANES), cols] = acc[g]
            return carry

        lax.fori_loop(0, ts // CONV_ROWS, row_block, 0)
    ext_scr[0:CONV_HALO, :] = ext_scr[ts:ts + CONV_HALO, :]
    y = y_scr[...] + bdw_ref[...]
    mu = jnp.mean(y, axis=-1, keepdims=True)
    yc = y - mu
    var = jnp.mean(yc * yc, axis=-1, keepdims=True)
    z = yc * lax.rsqrt(var + EPS) * lng_ref[...] + lnb_ref[...]
    z_ref[0] = (z * _sigmoid(z)).astype(BF16)


def _conv_dw(a, w_dw, b_dw, ln_g, ln_b, ts):
    B, S, D = a.shape

    def full(x):
        return pl.BlockSpec(x.shape, lambda b, s: (0,) * x.ndim)

    tile = pl.BlockSpec((1, ts, D), lambda b, s: (b, s, 0))
    args = (w_dw, b_dw[None, :], ln_g[None, :], ln_b[None, :])
    return pl.pallas_call(
        functools.partial(_conv_dw_kernel, ts=ts),
        out_shape=jax.ShapeDtypeStruct((B, S, D), BF16),
        grid=(B, S // ts),
        in_specs=[tile] + [full(x) for x in args],
        out_specs=tile,
        scratch_shapes=[pltpu.VMEM((ts + CONV_HALO, D), F32), pltpu.VMEM((ts, D), F32),
                        pltpu.VMEM((SUBLANES - 1, ts + CONV_HALO - SUBLANES, D), F32)],
        compiler_params=_cparams("parallel", "arbitrary"),
        name="conv_depthwise",
    )(a, *args)


def _lower_bounds_kernel(x_ref, o_ref):
    x = x_ref[...]
    n = x.shape[0]
    m = x[0:1]
    for r in range(1, n):
        m = jnp.maximum(m, x[r:r + 1])
    e = jnp.exp(x - m)
    tot = e[0:1]
    for r in range(1, n):
        tot = tot + e[r:r + 1]
    sm = e / tot
    cum = sm[0:1]
    first = cum
    o_ref[0:1, :] = cum - first
    for r in range(1, n):
        cum = cum + sm[r:r + 1]
        o_ref[r:r + 1, :] = cum - first


def _lower_bounds(logits):
    return pl.pallas_call(
        _lower_bounds_kernel,
        out_shape=jax.ShapeDtypeStruct(logits.shape, F32),
        name="hgrn_lower_bounds",
    )(logits)


SUB = SUBLANES
NGRP = CHUNK // SUB
HGRN_LEVELS = (32, 16, 8, 4, 2)
HGRN_DIRECT = 2
HGRN_PACK = 8


def _groups(x):
    return [x[g * SUB:(g + 1) * SUB, :] for g in range(NGRP)]


def _cumsum_chunk(x, row):
    gs = _groups(x)
    for step in (1, 2, 4):
        gs = [g + jnp.where(row >= step, pltpu.roll(g, step, axis=0), 0.0) for g in gs]
    out = [gs[0]]
    run = gs[0][SUB - 1:SUB, :]
    for g in range(1, NGRP):
        out.append(gs[g] + run)
        run = run + gs[g][SUB - 1:SUB, :]
    return jnp.concatenate(out, axis=0)


def _level_operands(m, bg, qg, kg, row):
    zeros = jnp.zeros((SUB, LANES), F32)
    qs, ks = [], []
    for g in range(NGRP):
        if m >= SUB:
            blk = (g * SUB) // m
            if blk % 2 == 1:
                ref = bg[blk * m // SUB - 1][SUB - 1:SUB, :]
                qs.append(qg[g] * jnp.exp2(bg[g] - ref))
                ks.append(zeros)
            else:
                ref = bg[(blk + 1) * m // SUB - 1][SUB - 1:SUB, :]
                qs.append(zeros)
                ks.append(kg[g] * jnp.exp2(ref - bg[g]))
        else:
            b = bg[g]
            if 2 * m == SUB:
                ref = b[m - 1:m, :]
            else:
                ref = jnp.where(row >= SUB // 2, b[SUB // 2 + m - 1:SUB // 2 + m, :], b[m - 1:m, :])
            odd = (row & m) != 0
            t = b - ref
            e = jnp.exp2(jnp.where(odd, t, -t))
            qs.append(jnp.where(odd, qg[g] * e, 0.0))
            ks.append(jnp.where(odd, 0.0, kg[g] * e))
    return jnp.concatenate(qs, axis=0).astype(BF16), jnp.concatenate(ks, axis=0).astype(BF16)


def _hgrn_chunk(qs, lfs, vs, gates, gains, sts, masks):
    C = CHUNK
    row, pair_masks, diag_masks = masks
    heads = range(len(qs))
    fs = [jnp.exp2(lfs[h]) for h in heads]
    kks = [1.0 - fs[h] for h in heads]
    bs = [_cumsum_chunk(lfs[h], row) for h in heads]
    b_lasts = [bs[h][C - 1:C, :] for h in heads]
    vbs = [vs[h].astype(BF16) for h in heads]

    os_ = [_dot_nt((qs[h] * jnp.exp2(bs[h])).astype(BF16), sts[h].astype(BF16)) for h in heads]
    kds = [(kks[h] * jnp.exp2(b_lasts[h] - bs[h])).astype(BF16) for h in heads]
    st_news = [sts[h] * jnp.exp2(b_lasts[h]) + _dot_tn(vbs[h], kds[h]) for h in heads]

    bgs = [_groups(bs[h]) for h in heads]
    qgs = [_groups(qs[h]) for h in heads]
    kgs = [_groups(kks[h]) for h in heads]
    fgs = [_groups(fs[h]) for h in heads]
    attns = [jnp.zeros((C, C), F32) for _ in heads]
    for m, pair in zip(HGRN_LEVELS, pair_masks):
        ops = [_level_operands(m, bgs[h], qgs[h], kgs[h], row) for h in heads]
        for h in heads:
            attns[h] = jnp.where(pair, _dot_nt(*ops[h]), attns[h])

    for d in range(HGRN_DIRECT):
        for h in heads:
            if d == 0:
                w = qs[h] * kks[h]
            else:
                w = jnp.concatenate(
                    [qgs[h][g] * fgs[h][g] * pltpu.roll(kgs[h][g], d, axis=0)
                     for g in range(NGRP)], axis=0)
            val = jnp.sum(w, axis=1, keepdims=True)
            attns[h] = jnp.where(diag_masks[d], val, attns[h])

    outs = []
    for h in heads:
        o = os_[h] + _dot(attns[h].astype(BF16), vbs[h])
        outs.append(_rms(o, gains[h]) * _sigmoid(gates[h]))
    return outs, st_news


def _hgrn_scan_kernel(q_ref, lf_ref, v_ref, gate_ref, ng_ref, o_ref, st_scr, *, ts):
    @pl.when(pl.program_id(2) == 0)
    def _():
        st_scr[...] = jnp.zeros(st_scr.shape, F32)

    C = CHUNK
    hd = HGRN_HEAD_DIM
    lanes = [slice(h * hd, (h + 1) * hd) for h in range(HGRN_PACK)]
    gains = [ng_ref[:, sl] for sl in lanes]
    row = lax.broadcasted_iota(jnp.int32, (SUB, LANES), 0)
    trow = lax.broadcasted_iota(jnp.int32, (C, C), 0)
    scol = lax.broadcasted_iota(jnp.int32, (C, C), 1)
    pair_masks = []
    for m in HGRN_LEVELS:
        sh = m.bit_length() - 1
        pair_masks.append((((trow >> sh) & 1) == 1) & ((scol >> sh) == (trow >> sh) - 1))
    diag_masks = [(trow - scol == d) & ((trow & (HGRN_DIRECT - 1)) >= d) for d in range(HGRN_DIRECT)]
    masks = (row, pair_masks, diag_masks)

    def body(c, carry):
        r0 = pl.multiple_of(c * C, C)
        rows = pl.ds(r0, C)
        outs, st_news = _hgrn_chunk(
            [q_ref[0, rows, sl].astype(F32) for sl in lanes], [lf_ref[0, rows, sl] for sl in lanes],
            [v_ref[0, rows, sl] for sl in lanes], [gate_ref[0, rows, sl].astype(F32) for sl in lanes],
            gains, [st_scr[h] for h in range(HGRN_PACK)], masks)
        for h, sl in enumerate(lanes):
            st_scr[h] = st_news[h]
            o_ref[0, rows, sl] = outs[h].astype(BF16)
        return carry

    lax.fori_loop(0, ts // C, body, 0)


def _hgrn_scan(qig, lf, norm_g, ts):
    B, S, _ = lf.shape
    H = HGRN_HEADS
    hd = HGRN_HEAD_DIM
    w = HGRN_PACK * hd
    nblk = H // HGRN_PACK

    def col(section):
        return pl.BlockSpec((1, ts, w), lambda b, h, s: (b, s, h + section * nblk))

    vec = pl.BlockSpec((1, w), lambda b, h, s: (0, h))
    return pl.pallas_call(
        functools.partial(_hgrn_scan_kernel, ts=ts),
        out_shape=jax.ShapeDtypeStruct((B, S, H * hd), BF16),
        grid=(B, nblk, S // ts),
        in_specs=[col(0), col(0), col(1), col(2), vec],
        out_specs=pl.BlockSpec((1, ts, w), lambda b, h, s: (b, s, h)),
        scratch_shapes=[pltpu.VMEM((HGRN_PACK, hd, hd), F32)],
        compiler_params=_cparams("parallel", "parallel", "arbitrary"),
        name="hgrn_scan",
    )(qig, lf, qig, qig, norm_g[None, :])


def _tile(n, want):
    t = min(n, want)
    assert n % t == 0, (n, want)
    return t


def kernel(x, p, positions, norm1_g, norm2_g, mla_w_in, mla_q_norm_g, mla_w_uq, mla_kv_norm_g, mla_w_ukv, mla_w_out, conv_w_pw1, conv_b_pw1, conv_w_dw, conv_b_dw, conv_ln_g, conv_ln_b, conv_w_pw2, conv_b_pw2, hgrn_w_in, hgrn_lb_logits, hgrn_norm_g, hgrn_w_out, ffn_w_gu, ffn_w_down, ple_w_proj, ple_norm_g, ple_w_gate, final_norm_g):
    B, S, D = x.shape
    depth = norm1_g.shape[0]
    T = B * S
    tm = _tile(T, 512)
    ts_proj = _tile(S, 512)
    tq = _tile(S, 512)
    ts_conv = _tile(S, 512)
    ts_scan = _tile(S, 1024)
    zero_bias = jnp.zeros((D,), F32)

    lower_bounds = _lower_bounds(hgrn_lb_logits.astype(F32))
    cos, sin = _rope_tables(positions, _tile(S, 512))

    h = x.reshape(T, D)
    p_all = p.reshape(depth * T, PLE_DIM)
    for i in range(depth):
        mixer = i % N_MIXERS
        j = i // N_MIXERS
        if mixer == 0:
            q, k, vt = _mla_proj(h.reshape(B, S, D), norm1_g[i], mla_w_in[j], mla_q_norm_g[j], mla_w_uq[j],
                                 mla_kv_norm_g[j], mla_w_ukv[j], cos, sin, ts_proj)
            a = _attention(q, k, vt, tq, tq)
            w_o, b_o = mla_w_out[j], zero_bias
        elif mixer == 1:
            glu = _conv_glu(h, norm1_g[i], conv_w_pw1[j], conv_b_pw1[j], _tile(T, 1024), 512)
            a = _conv_dw(glu.reshape(B, S, D), conv_w_dw[j], conv_b_dw[j], conv_ln_g[j], conv_ln_b[j], ts_conv)
            w_o, b_o = conv_w_pw2[j], conv_b_pw2[j]
        else:
            qig, lf = _hgrn_proj(h, norm1_g[i], hgrn_w_in[j], lower_bounds[i:i + 1], _tile(T, 1024))
            a = _hgrn_scan(qig.reshape(B, S, 3 * D), lf.reshape(B, S, D), hgrn_norm_g[j], ts_scan)
            w_o, b_o = hgrn_w_out[j], zero_bias
        h = _layer_tail(h, a.reshape(T, D), w_o, b_o, norm2_g[i], ffn_w_gu[i], ffn_w_down[i],
                        p_all, i, ple_norm_g[i], ple_w_gate[i], ple_w_proj[i],
                        final_norm_g, i == depth - 1, _tile(T, 1024))
    return h.reshape(B, S, D)
```
